```python
import math
import jax
import jax.numpy as jnp
from jax import lax
import numpy as np

D_MODEL = 1024
BATCH = 2
SEQ = 8192
DEPTH = 2

GRID_W = 64
HEAD_DIM = 64
N_HEADS_TOTAL = D_MODEL // HEAD_DIM
GROUP_HEADS = N_HEADS_TOTAL // 4
GROUP_KV_HEADS = GROUP_HEADS // 2
GQA_GROUP = GROUP_HEADS // GROUP_KV_HEADS
MIX_WIDTH = 4 * GROUP_HEADS * HEAD_DIM
ROPE_THETA = 10000.0
NORM_EPS = 1e-6
Q_BLOCK = 128
NEG_INF = -1e30

NA_WIN_ROWS = 8
NA_WIN_COLS = 16

MLA_Q_RANK = D_MODEL // 4
MLA_KV_RANK = D_MODEL // 8
MLA_NOPE = HEAD_DIM
MLA_ROPE = HEAD_DIM // 2
MLA_V = HEAD_DIM

SW_WINDOW = 128

D_FF = 2816
CONV_W = 3

SPLIT_SIZES = (
    GROUP_HEADS * HEAD_DIM, GROUP_HEADS * HEAD_DIM, GROUP_HEADS * HEAD_DIM,
    MLA_Q_RANK, MLA_KV_RANK, MLA_ROPE,
    GROUP_HEADS * HEAD_DIM, GROUP_KV_HEADS * HEAD_DIM, GROUP_KV_HEADS * HEAD_DIM,
    GROUP_HEADS * HEAD_DIM, GROUP_KV_HEADS * HEAD_DIM, GROUP_KV_HEADS * HEAD_DIM,
)
IN_COLS = sum(SPLIT_SIZES)

kernel_name = 'hybrid_parallel_head_group_encoder'


def rms_norm(x, gain):
    xf = x.astype(jnp.float32)
    y = xf * lax.rsqrt(jnp.mean(xf * xf, axis=-1, keepdims=True) + NORM_EPS) * gain.astype(jnp.float32)
    return y.astype(x.dtype)


def rope_angles(pos, dim):
    inv = ROPE_THETA ** (-jnp.arange(0, dim, 2, dtype=jnp.float32) / dim)
    return pos.astype(jnp.float32)[:, None] * inv[None, :]


def apply_rope(x, ang):
    cos = jnp.cos(ang)[None, :, None, :]
    sin = jnp.sin(ang)[None, :, None, :]
    x1, x2 = jnp.split(x.astype(jnp.float32), 2, axis=-1)
    out = jnp.concatenate([x1 * cos - x2 * sin, x2 * cos + x1 * sin], axis=-1)
    return out.astype(x.dtype)


def axial_rope(x, ang_row, ang_col):
    half = x.shape[-1] // 2
    return jnp.concatenate([apply_rope(x[..., :half], ang_row), apply_rope(x[..., half:], ang_col)], axis=-1)


def split_columns(z):
    points = []
    acc = 0
    for n in SPLIT_SIZES[:-1]:
        acc += n
        points.append(acc)
    return jnp.split(z, points, axis=-1)


def dense_attention_blocked(q, k, v, scale):
    b, s, hkv, g, dq = q.shape
    nb = s // Q_BLOCK
    qb = q.reshape(b, nb, Q_BLOCK, hkv, g, dq).transpose(1, 0, 2, 3, 4, 5)

    def one_block(q_blk):
        sc = jnp.einsum('bqhgd,bkhd->bhgqk', q_blk, k).astype(jnp.float32) * scale
        p = jax.nn.softmax(sc, axis=-1).astype(v.dtype)
        return jnp.einsum('bhgqk,bkhd->bqhgd', p, v)

    out = lax.map(one_block, qb)
    return out.transpose(1, 0, 2, 3, 4, 5).reshape(b, s, hkv, g, v.shape[-1])


def sliding_window_sink_attention(q, k, v, sink, scale):
    b, s, hkv, g, d = q.shape
    nb = s // Q_BLOCK
    pad = ((0, 0), (Q_BLOCK, Q_BLOCK), (0, 0), (0, 0))
    kp = jnp.pad(k, pad).reshape(b, nb + 2, Q_BLOCK, hkv, d)
    vp = jnp.pad(v, pad).reshape(b, nb + 2, Q_BLOCK, hkv, d)
    kb = jnp.concatenate([kp[:, :-2], kp[:, 1:-1], kp[:, 2:]], axis=2)
    vb = jnp.concatenate([vp[:, :-2], vp[:, 1:-1], vp[:, 2:]], axis=2)
    qb = q.reshape(b, nb, Q_BLOCK, hkv, g, d)
    sc = jnp.einsum('bnqhgd,bnkhd->bnhgqk', qb, kb).astype(jnp.float32) * scale
    blk = jnp.arange(nb)[:, None] * Q_BLOCK
    qpos = blk + jnp.arange(Q_BLOCK)[None, :]
    kpos = blk - Q_BLOCK + jnp.arange(3 * Q_BLOCK)[None, :]
    valid = (jnp.abs(qpos[:, :, None] - kpos[:, None, :]) <= SW_WINDOW) & ((kpos >= 0) & (kpos < s))[:, None, :]
    sc = jnp.where(valid[None, :, None, None], sc, NEG_INF)
    sink_l = jnp.broadcast_to(sink.reshape(hkv, g).astype(jnp.float32)[None, None, :, :, None, None],
                              sc.shape[:-1] + (1,))
    p = jax.nn.softmax(jnp.concatenate([sc, sink_l], axis=-1), axis=-1)[..., :-1]
    out = jnp.einsum('bnhgqk,bnkhd->bnqhgd', p.astype(v.dtype), vb)
    return out.reshape(b, s, hkv, g, d)


def neighbourhood_attention_2d(q, k, v, rpb):
    b, s, h, d = q.shape
    rows = s // GRID_W
    kr = min(NA_WIN_ROWS, rows)
    kc = NA_WIN_COLS
    qg = q.reshape(b, rows, GRID_W, h, d)
    kg = k.reshape(b, rows, GRID_W, h, d)
    vg = v.reshape(b, rows, GRID_W, h, d)
    col = jnp.arange(GRID_W)
    col_start = jnp.clip(col - kc // 2, 0, GRID_W - kc)
    col_idx = col_start[:, None] + jnp.arange(kc)[None, :]
    col_off = col_idx - col[:, None] + (NA_WIN_COLS - 1)
    row_ids = jnp.arange(rows)
    row_start = jnp.clip(row_ids - kr // 2, 0, rows - kr)
    scale = HEAD_DIM ** -0.5

    def one_row(args):
        r, rs = args
        q_r = lax.dynamic_index_in_dim(qg, r, axis=1, keepdims=False)
        k_band = lax.dynamic_slice_in_dim(kg, rs, kr, axis=1)
        v_band = lax.dynamic_slice_in_dim(vg, rs, kr, axis=1)
        k_nb = k_band[:, :, col_idx]
        v_nb = v_band[:, :, col_idx]
        row_off = rs + jnp.arange(kr) - r + (NA_WIN_ROWS - 1)
        bias = rpb[:, row_off[:, None, None], col_off[None, :, :]]
        sc = jnp.einsum('bchd,bacwhd->bhcaw', q_r, k_nb).astype(jnp.float32) * scale
        sc = sc + bias.transpose(0, 2, 1, 3)[None].astype(jnp.float32)
        p = jax.nn.softmax(sc.reshape(b, h, GRID_W, kr * kc), axis=-1)
        p = p.reshape(b, h, GRID_W, kr, kc).astype(v.dtype)
        return jnp.einsum('bhcaw,bacwhd->bchd', p, v_nb)

    out = lax.map(one_row, (row_ids, row_start))
    return out.transpose(1, 0, 2, 3, 4).reshape(b, s, h, d)


def conv_gated_mlp(h, w_up, conv_w, conv_b, w_down):
    u = h @ w_up
    s = u.shape[1]
    half = CONV_W // 2
    up = jnp.pad(u, ((0, 0), (half, half), (0, 0)))
    c = conv_b
    for i in range(CONV_W):
        c = c + up[:, i:i + s] * conv_w[i]
    gate, val = jnp.split(c, 2, axis=-1)
    return (jax.nn.gelu(gate, approximate=True) * val) @ w_down


def setup_inputs(seed: int = 0) -> dict:
    key = jax.random.key(seed)
    ks = jax.random.split(key, 20)
    f32 = jnp.float32

    def dense(k, shape, fan_in):
        return jax.random.normal(k, shape, f32) * fan_in ** -0.5

    def gain(k, n):
        return 1.0 + 0.05 * jax.random.normal(k, (DEPTH, n), f32)

    return {
        'x': jax.random.normal(ks[0], (BATCH, SEQ, D_MODEL), f32),
        'mix_pre_gain': gain(ks[1], D_MODEL),
        'w_in': dense(ks[2], (DEPTH, D_MODEL, IN_COLS), D_MODEL),
        'na_rpb': 0.1 * jax.random.normal(ks[3], (DEPTH, GROUP_HEADS, 2 * NA_WIN_ROWS - 1, 2 * NA_WIN_COLS - 1), f32),
        'mla_q_gain': gain(ks[4], MLA_Q_RANK),
        'mla_w_uq': dense(ks[5], (DEPTH, MLA_Q_RANK, GROUP_HEADS * (MLA_NOPE + MLA_ROPE)), MLA_Q_RANK),
        'mla_kv_gain': gain(ks[6], MLA_KV_RANK),
        'mla_w_ukv': dense(ks[7], (DEPTH, MLA_KV_RANK, GROUP_HEADS * (MLA_NOPE + MLA_V)), MLA_KV_RANK),
        'ax_q_gain': gain(ks[8], HEAD_DIM),
        'ax_k_gain': gain(ks[9], HEAD_DIM),
        'sw_sink': 0.5 * jax.random.normal(ks[10], (DEPTH, GROUP_HEADS), f32),
        'w_out': dense(ks[11], (DEPTH, MIX_WIDTH, D_MODEL), MIX_WIDTH),
        'mix_post_gain': gain(ks[12], D_MODEL),
        'ffn_pre_gain': gain(ks[13], D_MODEL),
        'w_up': dense(ks[14], (DEPTH, D_MODEL, 2 * D_FF), D_MODEL),
        'conv_w': dense(ks[15], (DEPTH, CONV_W, 2 * D_FF), CONV_W),
        'conv_b': 0.01 * jax.random.normal(ks[16], (DEPTH, 2 * D_FF), f32),
        'w_down': dense(ks[17], (DEPTH, D_FF, D_MODEL), D_FF),
        'ffn_post_gain': gain(ks[18], D_MODEL),
    }


def reference(x, mix_pre_gain, w_in, na_rpb, mla_q_gain, mla_w_uq, mla_kv_gain, mla_w_ukv,
              ax_q_gain, ax_k_gain, sw_sink, w_out, mix_post_gain, ffn_pre_gain, w_up,
              conv_w, conv_b, w_down, ffn_post_gain):
    b, s, _ = x.shape
    t = jnp.arange(s)
    ang_full = rope_angles(t, HEAD_DIM)
    ang_mla = rope_angles(t, MLA_ROPE)
    ang_row = rope_angles(t // GRID_W, HEAD_DIM // 2)
    ang_col = rope_angles(t % GRID_W, HEAD_DIM // 2)

    def heads(z, n):
        return z.reshape(b, s, n, -1)

    for l in range(DEPTH):
        h = rms_norm(x, mix_pre_gain[l])
        (a_q, a_k, a_v, b_cq, b_ckv, b_kr,
         c_q, c_k, c_v, d_q, d_k, d_v) = split_columns(h @ w_in[l])

        o_a = neighbourhood_attention_2d(heads(a_q, GROUP_HEADS), heads(a_k, GROUP_HEADS),
                                         heads(a_v, GROUP_HEADS), na_rpb[l])

        q_b = (rms_norm(b_cq, mla_q_gain[l]) @ mla_w_uq[l]).reshape(b, s, GROUP_HEADS, MLA_NOPE + MLA_ROPE)
        q_nope, q_pe = jnp.split(q_b, [MLA_NOPE], axis=-1)
        q_b = jnp.concatenate([q_nope, apply_rope(q_pe, ang_mla)], axis=-1)
        kv_b = (rms_norm(b_ckv, mla_kv_gain[l]) @ mla_w_ukv[l]).reshape(b, s, GROUP_HEADS, MLA_NOPE + MLA_V)
        k_nope, v_b = jnp.split(kv_b, [MLA_NOPE], axis=-1)
        k_pe = apply_rope(b_kr.reshape(b, s, 1, MLA_ROPE), ang_mla)
        k_b = jnp.concatenate([k_nope, jnp.broadcast_to(k_pe, (b, s, GROUP_HEADS, MLA_ROPE))], axis=-1)
        o_b = dense_attention_blocked(q_b[:, :, :, None, :], k_b, v_b, (MLA_NOPE + MLA_ROPE) ** -0.5)

        q_c = axial_rope(rms_norm(heads(c_q, GROUP_HEADS), ax_q_gain[l]), ang_row, ang_col)
        k_c = axial_rope(rms_norm(heads(c_k, GROUP_KV_HEADS), ax_k_gain[l]), ang_row, ang_col)
        o_c = dense_attention_blocked(q_c.reshape(b, s, GROUP_KV_HEADS, GQA_GROUP, HEAD_DIM), k_c,
                                      heads(c_v, GROUP_KV_HEADS), HEAD_DIM ** -0.5)

        q_d = apply_rope(heads(d_q, GROUP_HEADS), ang_full).reshape(b, s, GROUP_KV_HEADS, GQA_GROUP, HEAD_DIM)
        k_d = apply_rope(heads(d_k, GROUP_KV_HEADS), ang_full)
        o_d = sliding_window_sink_attention(q_d, k_d, heads(d_v, GROUP_KV_HEADS), sw_sink[l], HEAD_DIM ** -0.5)

        mixed = jnp.concatenate([o_a.reshape(b, s, -1), o_b.reshape(b, s, -1),
                                 o_c.reshape(b, s, -1), o_d.reshape(b, s, -1)], axis=-1) @ w_out[l]
        x = x + rms_norm(mixed, mix_post_gain[l])

        h = rms_norm(x, ffn_pre_gain[l])
        y = conv_gated_mlp(h, w_up[l], conv_w[l], conv_b[l], w_down[l])
        x = x + rms_norm(y, ffn_post_gain[l])
    return x
```

```python
import functools
import math

import numpy as np
import jax
import jax.numpy as jnp
from jax import lax
from jax.experimental import pallas as pl
from jax.experimental.pallas import tpu as pltpu

F32 = jnp.float32
BF16 = jnp.bfloat16

D_MODEL = 1024
GRID_W = 64
HEAD_DIM = 64
GROUP_HEADS = 4
GROUP_KV_HEADS = 2
GROUP_WIDTH = GROUP_HEADS * HEAD_DIM
ROPE_THETA = 10000.0
NORM_EPS = 1e-6
MASK_VALUE = -1e30
NA_WIN_ROWS = 8
NA_WIN_COLS = 16
MLA_Q_RANK = 256
MLA_KV_RANK = 128
MLA_NOPE = 64
MLA_ROPE = 32
SW_WINDOW = 128
D_FF = 2816
LOG2E = math.log2(math.e)

LANES = 128
VMEM_LIMIT = 56 * 1024 * 1024

TOKEN_TILE = 512
FLASH_TQ = 512
NA_ROWS_PER_STEP = 8
SW_TQ = 256
FF_CHUNK = 256


def _rms(x, gain):
    return x * lax.rsqrt(jnp.mean(x * x, axis=-1, keepdims=True) + NORM_EPS) * gain


def _rope_lanes(x, tab_ref, half):
    w = x.shape[-1]
    return (x * tab_ref[0] + pltpu.roll(x, w - half, 1) * tab_ref[1]
            + pltpu.roll(x, half, 1) * tab_ref[2])


def _bdot(a, b):
    return jnp.dot(a, b, preferred_element_type=F32)


_C_AQ, _C_AK, _C_AV = 0, 256, 512
_C_BCQ, _C_BCKV = 768, 1024
_C_CQ, _C_CK, _C_CV = 1152, 1408, 1536
_C_DQ, _C_DK, _C_DV = 1664, 1920, 2048
_C_BKR = 2176
_IN_COLS_PADDED = 2304


def _proj_kernel(x_ref, g_ref, win_ref, qg_ref, wuq_ref, kvg_ref, wuk_ref, wuv_ref,
                 cqg_ref, ckg_ref, tabb_ref, tabd_ref, tabc_ref,
                 aq_ref, ak_ref, av_ref, bqT_ref, bk_ref, bvT_ref,
                 cqT_ref, ck_ref, cvT_ref, dq_ref, dk_ref, dv_ref):
    h = _rms(x_ref[0], g_ref[...]).astype(BF16)

    def proj(c0, width):
        return _bdot(h, win_ref[:, c0:c0 + width])

    def store_heads(ref, z, n_heads):
        for hd in range(n_heads):
            ref[0, hd] = z[:, hd * HEAD_DIM:(hd + 1) * HEAD_DIM].astype(BF16)

    store_heads(aq_ref, proj(_C_AQ, GROUP_WIDTH), GROUP_HEADS)
    store_heads(ak_ref, proj(_C_AK, GROUP_WIDTH), GROUP_HEADS)
    store_heads(av_ref, proj(_C_AV, GROUP_WIDTH), GROUP_HEADS)

    cq = _rms(proj(_C_BCQ, MLA_Q_RANK), qg_ref[...]).astype(BF16)
    qb = _bdot(cq, wuq_ref[...]) * ((MLA_NOPE + MLA_ROPE) ** -0.5 * LOG2E)
    kpe = _rope_lanes(proj(_C_BKR, LANES), tabb_ref, MLA_ROPE // 2)
    ckv = _rms(proj(_C_BCKV, MLA_KV_RANK), kvg_ref[...]).astype(BF16)
    kn = _bdot(ckv, wuk_ref[...])
    for hd in range(GROUP_HEADS):
        blk = slice(hd * LANES, (hd + 1) * LANES)
        qh = _rope_lanes(qb[:, blk], tabb_ref, MLA_ROPE // 2)
        bqT_ref[0, blk, :] = qh.T.astype(BF16)
        bk_ref[0, :, blk] = (kn[:, blk] + kpe).astype(BF16)
    bvT_ref[0, 0] = _bdot(ckv, wuv_ref[...]).T.astype(BF16)

    def norm_rope_T(blk, gain_col):
        ms = jnp.mean(blk * blk, axis=0, keepdims=True)
        blk = blk * lax.rsqrt(ms + NORM_EPS) * gain_col
        q = HEAD_DIM // 4
        cr, sr = tabc_ref[0:q], tabc_ref[q:2 * q]
        cc, sc = tabc_ref[2 * q:3 * q], tabc_ref[3 * q:4 * q]
        x1, x2, x3, x4 = blk[0:q], blk[q:2 * q], blk[2 * q:3 * q], blk[3 * q:4 * q]
        return jnp.concatenate([x1 * cr - x2 * sr, x2 * cr + x1 * sr,
                                x3 * cc - x4 * sc, x4 * cc + x3 * sc], axis=0)

    cqT = proj(_C_CQ, GROUP_WIDTH).T
    for hd in range(GROUP_HEADS):
        rows = slice(hd * HEAD_DIM, (hd + 1) * HEAD_DIM)
        cqT_ref[0, rows, :] = norm_rope_T(cqT[rows], cqg_ref[...]).astype(BF16)
    ckT = proj(_C_CK, GROUP_KV_HEADS * HEAD_DIM).T
    ck = jnp.concatenate(
        [norm_rope_T(ckT[hd * HEAD_DIM:(hd + 1) * HEAD_DIM], ckg_ref[...])
         for hd in range(GROUP_KV_HEADS)], axis=0).T
    store_heads(ck_ref, ck, GROUP_KV_HEADS)
    cvT_ref[0, 0] = proj(_C_CV, GROUP_KV_HEADS * HEAD_DIM).T.astype(BF16)

    dq = proj(_C_DQ, GROUP_WIDTH)
    dq = jnp.concatenate([_rope_lanes(dq[:, j * LANES:(j + 1) * LANES], tabd_ref, HEAD_DIM // 2)
                          for j in range(GROUP_WIDTH // LANES)], axis=1)
    store_heads(dq_ref, dq, GROUP_HEADS)
    dk = _rope_lanes(proj(_C_DK, GROUP_KV_HEADS * HEAD_DIM), tabd_ref, HEAD_DIM // 2)
    store_heads(dk_ref, dk, GROUP_KV_HEADS)
    store_heads(dv_ref, proj(_C_DV, GROUP_KV_HEADS * HEAD_DIM), GROUP_KV_HEADS)


def _const_spec(shape):
    n = len(shape)
    return pl.BlockSpec(shape, lambda *_: (0,) * n)


def _projection(x, lw, tabs, tm):
    b, s, _ = x.shape
    nt = s // tm
    kvw = GROUP_KV_HEADS * HEAD_DIM
    head_q = jax.ShapeDtypeStruct((b, GROUP_HEADS, s, HEAD_DIM), BF16)
    head_kv = jax.ShapeDtypeStruct((b, GROUP_KV_HEADS, s, HEAD_DIM), BF16)
    out_shape = (
        head_q, head_q, head_q,
        jax.ShapeDtypeStruct((b, GROUP_HEADS * LANES, s), BF16),
        jax.ShapeDtypeStruct((b, s, GROUP_HEADS * LANES), BF16),
        jax.ShapeDtypeStruct((b, nt, GROUP_WIDTH, tm), BF16),
        jax.ShapeDtypeStruct((b, GROUP_WIDTH, s), BF16),
        head_kv,
        jax.ShapeDtypeStruct((b, nt, kvw, tm), BF16),
        head_q, head_kv, head_kv,
    )
    hq_spec = pl.BlockSpec((1, GROUP_HEADS, tm, HEAD_DIM), lambda bi, i: (bi, 0, i, 0))
    hkv_spec = pl.BlockSpec((1, GROUP_KV_HEADS, tm, HEAD_DIM), lambda bi, i: (bi, 0, i, 0))
    out_specs = (
        hq_spec, hq_spec, hq_spec,
        pl.BlockSpec((1, GROUP_HEADS * LANES, tm), lambda bi, i: (bi, 0, i)),
        pl.BlockSpec((1, tm, GROUP_HEADS * LANES), lambda bi, i: (bi, i, 0)),
        pl.BlockSpec((1, 1, GROUP_WIDTH, tm), lambda bi, i: (bi, i, 0, 0)),
        pl.BlockSpec((1, GROUP_WIDTH, tm), lambda bi, i: (bi, 0, i)),
        hkv_spec,
        pl.BlockSpec((1, 1, kvw, tm), lambda bi, i: (bi, i, 0, 0)),
        hq_spec, hkv_spec, hkv_spec,
    )
    in_specs = [
        pl.BlockSpec((1, tm, D_MODEL), lambda bi, i: (bi, i, 0)),
        _const_spec((1, D_MODEL)),
        _const_spec((D_MODEL, _IN_COLS_PADDED)),
        _const_spec((1, MLA_Q_RANK)),
        _const_spec((MLA_Q_RANK, GROUP_HEADS * LANES)),
        _const_spec((1, MLA_KV_RANK)),
        _const_spec((MLA_KV_RANK, GROUP_HEADS * LANES)),
        _const_spec((MLA_KV_RANK, GROUP_WIDTH)),
        _const_spec((HEAD_DIM, 1)),
        _const_spec((HEAD_DIM, 1)),
        pl.BlockSpec((3, tm, LANES), lambda bi, i: (0, i, 0)),
        pl.BlockSpec((3, tm, LANES), lambda bi, i: (0, i, 0)),
        pl.BlockSpec((HEAD_DIM, tm), lambda bi, i: (0, i)),
    ]
    return pl.pallas_call(
        _proj_kernel,
        grid=(b, nt),
        in_specs=in_specs,
        out_specs=out_specs,
        out_shape=out_shape,
        compiler_params=pltpu.CompilerParams(
            dimension_semantics=("parallel", "parallel"), vmem_limit_bytes=VMEM_LIMIT),
        name="projection",
    )(x, lw["pre_gain"], lw["w_in"], lw["q_gain"], lw["w_uq"], lw["kv_gain"], lw["w_uk"],
      lw["w_uv"], lw["cq_gain"], lw["ck_gain"], tabs["mla"], tabs["full"], tabs["axial"])


def _flash_kernel(qT_ref, k_ref, vT_ref, oT_ref, *, tk, n_chunks):
    qT = qT_ref[0]
    tq = qT.shape[1]
    dv = vT_ref.shape[2]

    def body(j, carry):
        m, l, acc = carry
        k = k_ref[0, pl.ds(pl.multiple_of(j * tk, tk), tk), :]
        s = _bdot(k, qT)
        m_new = jnp.maximum(m, jnp.max(s, axis=0, keepdims=True))
        alpha = jnp.exp2(m - m_new)
        p = jnp.exp2(s - m_new)
        l = alpha * l + jnp.sum(p, axis=0, keepdims=True)
        acc = alpha * acc + _bdot(vT_ref[0, j], p.astype(BF16))
        return m_new, l, acc

    init = (jnp.full((1, tq), -jnp.inf, F32), jnp.zeros((1, tq), F32), jnp.zeros((dv, tq), F32))
    _, l, acc = lax.fori_loop(0, n_chunks, body, init)
    oT_ref[0] = acc / l


def _flash(qT, k, vT, *, n_heads, n_kv, dk, k_head_major, tq):
    b, _, s = qT.shape
    n_chunks, tk = vT.shape[1], vT.shape[3]
    dv = HEAD_DIM
    rep = n_heads // n_kv
    if k_head_major:
        k_spec = pl.BlockSpec((None, 1, s, dk), lambda bi, h, i: (bi, h // rep, 0, 0))
    else:
        k_spec = pl.BlockSpec((1, s, dk), lambda bi, h, i: (bi, 0, h // rep))
    return pl.pallas_call(
        functools.partial(_flash_kernel, tk=tk, n_chunks=n_chunks),
        grid=(b, n_heads, s // tq),
        in_specs=[
            pl.BlockSpec((1, dk, tq), lambda bi, h, i: (bi, h, i)),
            k_spec,
            pl.BlockSpec((1, n_chunks, dv, tk), lambda bi, h, i: (bi, 0, h // rep, 0)),
        ],
        out_specs=pl.BlockSpec((1, dv, tq), lambda bi, h, i: (bi, h, i)),
        out_shape=jax.ShapeDtypeStruct((b, n_heads * dv, s), F32),
        compiler_params=pltpu.CompilerParams(
            dimension_semantics=("parallel", "parallel", "parallel"),
            vmem_limit_bytes=VMEM_LIMIT),
        name="dense_attention",
    )(qT, k, vT)


def _na_kernel(q_ref, kp_ref, kc_ref, kn_ref, vp_ref, vc_ref, vn_ref, bias_ref, o_ref,
               k_win, v_win, *, n_rows):
    i = pl.program_id(1)
    blk = NA_ROWS_PER_STEP * GRID_W
    band = NA_WIN_ROWS * GRID_W
    for w, (kr, vr) in enumerate(((kp_ref, vp_ref), (kc_ref, vc_ref), (kn_ref, vn_ref))):
        k_win[:, w * blk:(w + 1) * blk, :] = kr[0]
        v_win[:, w * blk:(w + 1) * blk, :] = vr[0]
    for j in range(NA_ROWS_PER_STEP):
        r = i * NA_ROWS_PER_STEP + j
        rs = jnp.clip(r - NA_WIN_ROWS // 2, 0, n_rows - NA_WIN_ROWS)
        off = pl.multiple_of((rs - (i - 1) * NA_ROWS_PER_STEP) * GRID_W, GRID_W)
        d = r - rs
        for hd in range(GROUP_HEADS):
            q = q_ref[0, hd, j * GRID_W:(j + 1) * GRID_W, :]
            kb = k_win[hd, pl.ds(off, band), :]
            vb = v_win[hd, pl.ds(off, band), :]
            s = lax.dot_general(q, kb, (((1,), (1,)), ((), ())), preferred_element_type=F32)
            s = s + bias_ref[hd, d]
            p = jnp.exp(s - jnp.max(s, axis=-1, keepdims=True))
            o = _bdot(p.astype(BF16), vb) / jnp.sum(p, axis=-1, keepdims=True)
            o_ref[0, j * GRID_W:(j + 1) * GRID_W, hd * HEAD_DIM:(hd + 1) * HEAD_DIM] = o


def _neighbourhood(q, k, v, bias):
    b, nh, s, hd = q.shape
    blk = NA_ROWS_PER_STEP * GRID_W
    nb = s // blk
    n_rows = s // GRID_W
    cur = lambda bi, i: (bi, 0, i, 0)
    prev = lambda bi, i: (bi, 0, jnp.maximum(i - 1, 0), 0)
    nxt = lambda bi, i: (bi, 0, jnp.minimum(i + 1, nb - 1), 0)
    spec = lambda f: pl.BlockSpec((1, nh, blk, hd), f)
    return pl.pallas_call(
        functools.partial(_na_kernel, n_rows=n_rows),
        grid=(b, nb),
        in_specs=[spec(cur), spec(prev), spec(cur), spec(nxt), spec(prev), spec(cur), spec(nxt),
                  _const_spec(bias.shape)],
        out_specs=pl.BlockSpec((1, blk, nh * hd), lambda bi, i: (bi, i, 0)),
        out_shape=jax.ShapeDtypeStruct((b, s, nh * hd), F32),
        scratch_shapes=[pltpu.VMEM((nh, 3 * blk, hd), BF16), pltpu.VMEM((nh, 3 * blk, hd), BF16)],
        compiler_params=pltpu.CompilerParams(
            dimension_semantics=("parallel", "parallel"), vmem_limit_bytes=VMEM_LIMIT),
        name="neighbourhood_attention",
    )(q, k, k, k, v, v, v, bias)


def _na_bias_table(rpb):
    d = np.arange(NA_WIN_ROWS)[:, None]
    a = np.arange(NA_WIN_ROWS)[None, :]
    row_off = a - d + (NA_WIN_ROWS - 1)
    c = np.arange(GRID_W)[:, None]
    kc = np.arange(GRID_W)[None, :]
    cs = np.clip(c - NA_WIN_COLS // 2, 0, GRID_W - NA_WIN_COLS)
    valid = (kc >= cs) & (kc < cs + NA_WIN_COLS)
    col_off = np.clip(kc - c + (NA_WIN_COLS - 1), 0, 2 * NA_WIN_COLS - 2)
    t = rpb[:, row_off[:, None, :, None], col_off[None, :, None, :]]
    t = jnp.where(valid[None, None, :, None, :], t, MASK_VALUE)
    return t.reshape(rpb.shape[0], NA_WIN_ROWS, GRID_W, NA_WIN_ROWS * GRID_W).astype(F32)


def _sw_kernel(sink_ref, q_ref, kp_ref, kc_ref, kn_ref, vp_ref, vc_ref, vn_ref, o_ref, *, seq):
    i = pl.program_id(1)
    tq = q_ref.shape[2]
    span = tq + 2 * SW_WINDOW
    t0 = i * tq
    row = lax.broadcasted_iota(jnp.int32, (tq, span), 0)
    col = lax.broadcasted_iota(jnp.int32, (tq, span), 1)
    kpos = col + (t0 - SW_WINDOW)
    rel = col - row
    valid = (rel >= 0) & (rel <= 2 * SW_WINDOW) & (kpos >= 0) & (kpos < seq)
    nt = (((1,), (1,)), ((), ()))
    rep = GROUP_HEADS // GROUP_KV_HEADS
    for hd in range(GROUP_HEADS):
        g = hd // rep
        q = q_ref[0, hd]
        s = jnp.concatenate(
            [lax.dot_general(q, kr[0, g], nt, preferred_element_type=F32)
             for kr in (kp_ref, kc_ref, kn_ref)], axis=1)
        s = jnp.where(valid, s, MASK_VALUE)
        sink = sink_ref[hd]
        m = jnp.maximum(jnp.max(s, axis=-1, keepdims=True), sink)
        p = jnp.exp(s - m)
        denom = jnp.sum(p, axis=-1, keepdims=True) + jnp.exp(sink - m)
        pb = p.astype(BF16)
        o = (_bdot(pb[:, :SW_WINDOW], vp_ref[0, g])
             + _bdot(pb[:, SW_WINDOW:SW_WINDOW + tq], vc_ref[0, g])
             + _bdot(pb[:, SW_WINDOW + tq:], vn_ref[0, g]))
        o_ref[0, :, hd * HEAD_DIM:(hd + 1) * HEAD_DIM] = o / denom


def _sliding_window(q, k, v, sink, tq):
    b, nh, s, hd = q.shape
    nkv = k.shape[1]
    nb = s // tq
    r = tq // SW_WINDOW
    n_small = s // SW_WINDOW
    cur = pl.BlockSpec((1, nkv, tq, hd), lambda bi, i: (bi, 0, i, 0))
    prev = pl.BlockSpec((1, nkv, SW_WINDOW, hd), lambda bi, i: (bi, 0, jnp.maximum(i * r - 1, 0), 0))
    nxt = pl.BlockSpec((1, nkv, SW_WINDOW, hd),
                       lambda bi, i: (bi, 0, jnp.minimum((i + 1) * r, n_small - 1), 0))
    return pl.pallas_call(
        functools.partial(_sw_kernel, seq=s),
        grid=(b, nb),
        in_specs=[pl.BlockSpec(memory_space=pltpu.SMEM),
                  pl.BlockSpec((1, nh, tq, hd), lambda bi, i: (bi, 0, i, 0)),
                  prev, cur, nxt, prev, cur, nxt],
        out_specs=pl.BlockSpec((1, tq, nh * hd), lambda bi, i: (bi, i, 0)),
        out_shape=jax.ShapeDtypeStruct((b, s, nh * hd), F32),
        compiler_params=pltpu.CompilerParams(
            dimension_semantics=("parallel", "parallel"), vmem_limit_bytes=VMEM_LIMIT),
        name="sliding_window_attention",
    )(sink, q, k, k, k, v, v, v)


def _out_kernel(x_ref, oa_ref, obT_ref, ocT_ref, od_ref, w_ref, g_ref, o_ref):
    mixed_in = jnp.concatenate(
        [oa_ref[0].astype(BF16), obT_ref[0].T.astype(BF16), ocT_ref[0].T.astype(BF16),
         od_ref[0].astype(BF16)], axis=1)
    mixed = _bdot(mixed_in, w_ref[...])
    o_ref[0] = x_ref[0] + _rms(mixed, g_ref[...])


def _out_projection(x, o_a, o_bT, o_cT, o_d, w_out, gain, tm):
    b, s, _ = x.shape
    tok = lambda w: pl.BlockSpec((1, tm, w), lambda bi, i: (bi, i, 0))
    feat = pl.BlockSpec((1, GROUP_WIDTH, tm), lambda bi, i: (bi, 0, i))
    return pl.pallas_call(
        _out_kernel,
        grid=(b, s // tm),
        in_specs=[tok(D_MODEL), tok(GROUP_WIDTH), feat, feat, tok(GROUP_WIDTH),
                  _const_spec((D_MODEL, D_MODEL)), _const_spec((1, D_MODEL))],
        out_specs=tok(D_MODEL),
        out_shape=jax.ShapeDtypeStruct(x.shape, F32),
        compiler_params=pltpu.CompilerParams(
            dimension_semantics=("parallel", "parallel"), vmem_limit_bytes=VMEM_LIMIT),
        name="out_projection",
    )(x, o_a, o_bT, o_cT, o_d, w_out, gain)


FFN_HALO = 8


def _ffn_kernel(x_ref, xp_ref, xn_ref, g_ref, wg_ref, wv_ref, cwg_ref, cwv_ref, cbg_ref, cbv_ref,
                wd_ref, pg_ref, o_ref, h_scr, acc_scr, *, n_chunks):
    i = pl.program_id(1)
    n_tiles = pl.num_programs(1)
    tm = x_ref.shape[1]
    ext = tm + 2 * FFN_HALO
    g = g_ref[...]
    hp = _rms(xp_ref[0], g) * (i > 0).astype(F32)
    hn = _rms(xn_ref[0], g) * (i < n_tiles - 1).astype(F32)
    h_scr[...] = jnp.concatenate([hp, _rms(x_ref[0], g), hn], axis=0).astype(BF16)
    acc_scr[...] = jnp.zeros_like(acc_scr)

    def conv(u, cw, cb):
        u_prev = pltpu.roll(u, 1, 0)
        u_next = pltpu.roll(u, ext - 1, 0)
        c = cb + u_prev * cw[0:1] + u * cw[1:2] + u_next * cw[2:3]
        return c[FFN_HALO:FFN_HALO + tm]

    def body(c, carry):
        hh = h_scr[...]
        gate = conv(_bdot(hh, wg_ref[c]), cwg_ref[c], cbg_ref[c])
        val = conv(_bdot(hh, wv_ref[c]), cwv_ref[c], cbv_ref[c])
        act = jax.nn.gelu(gate, approximate=True) * val
        acc_scr[...] += _bdot(act.astype(BF16), wd_ref[c])
        return carry

    lax.fori_loop(0, n_chunks, body, 0)
    o_ref[0] = x_ref[0] + _rms(acc_scr[...], pg_ref[...])


def _ffn(x, lw, tm):
    b, s, _ = x.shape
    n_chunks, _, fc = lw["w_gate"].shape
    r = tm // FFN_HALO
    n_halo = s // FFN_HALO
    tile = pl.BlockSpec((1, tm, D_MODEL), lambda bi, i: (bi, i, 0))
    prev = pl.BlockSpec((1, FFN_HALO, D_MODEL), lambda bi, i: (bi, jnp.maximum(i * r - 1, 0), 0))
    nxt = pl.BlockSpec((1, FFN_HALO, D_MODEL),
                       lambda bi, i: (bi, jnp.minimum((i + 1) * r, n_halo - 1), 0))
    return pl.pallas_call(
        functools.partial(_ffn_kernel, n_chunks=n_chunks),
        grid=(b, s // tm),
        in_specs=[tile, prev, nxt, _const_spec((1, D_MODEL)),
                  _const_spec((n_chunks, D_MODEL, fc)), _const_spec((n_chunks, D_MODEL, fc)),
                  _const_spec((n_chunks, 3, fc)), _const_spec((n_chunks, 3, fc)),
                  _const_spec((n_chunks, 1, fc)), _const_spec((n_chunks, 1, fc)),
                  _const_spec((n_chunks, fc, D_MODEL)), _const_spec((1, D_MODEL))],
        out_specs=tile,
        out_shape=jax.ShapeDtypeStruct(x.shape, F32),
        scratch_shapes=[pltpu.VMEM((tm + 2 * FFN_HALO, D_MODEL), BF16),
                        pltpu.VMEM((tm, D_MODEL), F32)],
        compiler_params=pltpu.CompilerParams(
            dimension_semantics=("parallel", "parallel"), vmem_limit_bytes=VMEM_LIMIT),
        name="conv_mlp",
    )(x, x, x, lw["ffn_pre_gain"], lw["w_gate"], lw["w_val"], lw["cw_gate"], lw["cw_val"],
      lw["cb_gate"], lw["cb_val"], lw["w_down"], lw["ffn_post_gain"])


def _rope_tables(s):
    t = jnp.arange(s)

    def angles(pos, dim):
        inv = ROPE_THETA ** (-jnp.arange(0, dim, 2, dtype=F32) / dim)
        return pos.astype(F32)[:, None] * inv[None, :]

    def lane_table(ang, lead, trail, reps):
        half = ang.shape[1]
        cos, sin, zero = jnp.cos(ang), jnp.sin(ang), jnp.zeros_like(ang)
        one = lambda n: jnp.ones((s, n), F32)
        nul = lambda n: jnp.zeros((s, n), F32)
        c = jnp.concatenate([one(lead)] + [cos, cos] * reps + [one(trail)], axis=1)
        lo = jnp.concatenate([nul(lead)] + [-sin, zero] * reps + [nul(trail)], axis=1)
        hi = jnp.concatenate([nul(lead)] + [zero, sin] * reps + [nul(trail)], axis=1)
        assert c.shape[1] == LANES and 2 * half * reps + lead + trail == LANES
        return jnp.stack([c, lo, hi])

    ang_row = angles(t // GRID_W, HEAD_DIM // 2)
    ang_col = angles(t % GRID_W, HEAD_DIM // 2)
    return {
        "mla": lane_table(angles(t, MLA_ROPE), MLA_NOPE, LANES - MLA_NOPE - MLA_ROPE, 1),
        "full": lane_table(angles(t, HEAD_DIM), 0, 0, LANES // HEAD_DIM),
        "axial": jnp.concatenate([jnp.cos(ang_row), jnp.sin(ang_row),
                                  jnp.cos(ang_col), jnp.sin(ang_col)], axis=1).T,
    }


def _layer_weights(l, mix_pre_gain, w_in, na_rpb, mla_q_gain, mla_w_uq, mla_kv_gain, mla_w_ukv,
                   ax_q_gain, ax_k_gain, sw_sink, w_out, mix_post_gain, ffn_pre_gain, w_up,
                   conv_w, conv_b, w_down, ffn_post_gain):
    gw, kvw = GROUP_WIDTH, GROUP_KV_HEADS * HEAD_DIM
    sizes = (gw, gw, gw, MLA_Q_RANK, MLA_KV_RANK, MLA_ROPE, gw, kvw, kvw, gw, kvw, kvw)
    bounds = np.cumsum((0,) + sizes)
    (a_q, a_k, a_v, b_cq, b_ckv, b_kr, c_q, c_k, c_v, d_q, d_k, d_v) = [
        w_in[l][:, bounds[j]:bounds[j + 1]] for j in range(len(sizes))]
    scale = HEAD_DIM ** -0.5
    zeros = lambda n: jnp.zeros((D_MODEL, n), F32)
    kr_block = jnp.concatenate([zeros(MLA_NOPE), b_kr, zeros(LANES - MLA_NOPE - MLA_ROPE)], axis=1)
    w_in_r = jnp.concatenate([a_q * scale, a_k, a_v, b_cq, b_ckv, c_q, c_k, c_v,
                              d_q * scale, d_k, d_v, kr_block], axis=1)
    assert w_in_r.shape[1] == _IN_COLS_PADDED

    uq = mla_w_uq[l].reshape(MLA_Q_RANK, GROUP_HEADS, MLA_NOPE + MLA_ROPE)
    uq = jnp.pad(uq, ((0, 0), (0, 0), (0, LANES - MLA_NOPE - MLA_ROPE)))
    ukv = mla_w_ukv[l].reshape(MLA_KV_RANK, GROUP_HEADS, MLA_NOPE + HEAD_DIM)
    uk = jnp.pad(ukv[:, :, :MLA_NOPE], ((0, 0), (0, 0), (0, LANES - MLA_NOPE)))
    uv = ukv[:, :, MLA_NOPE:]

    n_chunks = D_FF // FF_CHUNK
    chunk_cols = lambda w: w.reshape(w.shape[0], n_chunks, FF_CHUNK).transpose(1, 0, 2)
    row = lambda v: v[None, :].astype(F32)
    return {
        "pre_gain": row(mix_pre_gain[l]),
        "w_in": w_in_r.astype(BF16),
        "q_gain": row(mla_q_gain[l]),
        "w_uq": uq.reshape(MLA_Q_RANK, GROUP_HEADS * LANES).astype(BF16),
        "kv_gain": row(mla_kv_gain[l]),
        "w_uk": uk.reshape(MLA_KV_RANK, GROUP_HEADS * LANES).astype(BF16),
        "w_uv": uv.reshape(MLA_KV_RANK, GROUP_WIDTH).astype(BF16),
        "cq_gain": (ax_q_gain[l] * (scale * LOG2E))[:, None].astype(F32),
        "ck_gain": ax_k_gain[l][:, None].astype(F32),
        "na_bias": _na_bias_table(na_rpb[l]),
        "sink": sw_sink[l].astype(F32),
        "w_out": w_out[l].astype(BF16),
        "post_gain": row(mix_post_gain[l]),
        "ffn_pre_gain": row(ffn_pre_gain[l]),
        "w_gate": chunk_cols(w_up[l][:, :D_FF]).astype(BF16),
        "w_val": chunk_cols(w_up[l][:, D_FF:]).astype(BF16),
        "cw_gate": chunk_cols(conv_w[l][:, :D_FF]),
        "cw_val": chunk_cols(conv_w[l][:, D_FF:]),
        "cb_gate": chunk_cols(conv_b[l][None, :D_FF]),
        "cb_val": chunk_cols(conv_b[l][None, D_FF:]),
        "w_down": w_down[l].reshape(n_chunks, FF_CHUNK, D_MODEL).astype(BF16),
        "ffn_post_gain": row(ffn_post_gain[l]),
    }


def kernel(x, mix_pre_gain, w_in, na_rpb, mla_q_gain, mla_w_uq, mla_kv_gain, mla_w_ukv, ax_q_gain,
           ax_k_gain, sw_sink, w_out, mix_post_gain, ffn_pre_gain, w_up, conv_w, conv_b, w_down,
           ffn_post_gain):
    b, s, d = x.shape
    assert d == D_MODEL and s % max(TOKEN_TILE, FLASH_TQ, NA_ROWS_PER_STEP * GRID_W, SW_TQ) == 0
    assert s // GRID_W >= NA_WIN_ROWS
    params = (mix_pre_gain, w_in, na_rpb, mla_q_gain, mla_w_uq, mla_kv_gain, mla_w_ukv, ax_q_gain,
              ax_k_gain, sw_sink, w_out, mix_post_gain, ffn_pre_gain, w_up, conv_w, conv_b, w_down,
              ffn_post_gain)
    tabs = _rope_tables(s)
    for l in range(w_in.shape[0]):
        lw = _layer_weights(l, *params)
        (a_q, a_k, a_v, b_qT, b_k, b_vT, c_qT, c_k, c_vT, d_q, d_k, d_v) = _projection(
            x, lw, tabs, TOKEN_TILE)
        o_a = _neighbourhood(a_q, a_k, a_v, lw["na_bias"])
        o_bT = _flash(b_qT, b_k, b_vT, n_heads=GROUP_HEADS, n_kv=GROUP_HEADS, dk=LANES,
                      k_head_major=False, tq=FLASH_TQ)
        o_cT = _flash(c_qT, c_k, c_vT, n_heads=GROUP_HEADS, n_kv=GROUP_KV_HEADS, dk=HEAD_DIM,
                      k_head_major=True, tq=FLASH_TQ)
        o_d = _sliding_window(d_q, d_k, d_v, lw["sink"], SW_TQ)
        x = _out_projection(x, o_a, o_bT, o_cT, o_d, lw["w_out"], lw["post_gain"], TOKEN_TILE)
        x = _ffn(x, lw, TOKEN_TILE)
    return x
```

```python
import functools
import math

import numpy as np
import jax
import jax.numpy as jnp
from jax import lax
from jax.experimental import pallas as pl
from jax.experimental.pallas import tpu as pltpu

F32 = jnp.float32
BF16 = jnp.bfloat16

D_MODEL = 1024
GRID_W = 64
HEAD_DIM = 64
GROUP_HEADS = 4
GROUP_KV_HEADS = 2
GROUP_WIDTH = GROUP_HEADS * HEAD_DIM
ROPE_THETA = 10000.0
NORM_EPS = 1e-6
MASK_VALUE = -1e30
NA_WIN_ROWS = 8
NA_WIN_COLS = 16
MLA_Q_RANK = 256
MLA_KV_RANK = 128
MLA_NOPE = 64
MLA_ROPE = 32
SW_WINDOW = 128
D_FF = 2816
LOG2E = math.log2(math.e)

LANES = 128
VMEM_LIMIT = 56 * 1024 * 1024

TOKEN_TILE = 512
FLASH_TQ = 512
FLASH_STREAMS = 2
V_EXT = HEAD_DIM + 16
NA_ROWS_PER_STEP = 8
SW_TQ = 256
FF_CHUNK = 256


def _rms(x, gain):
    return x * lax.rsqrt(jnp.mean(x * x, axis=-1, keepdims=True) + NORM_EPS) * gain


def _rope_lanes(x, tab_ref, half):
    w = x.shape[-1]
    return (x * tab_ref[0] + pltpu.roll(x, w - half, 1) * tab_ref[1]
            + pltpu.roll(x, half, 1) * tab_ref[2])


def _bdot(a, b):
    return jnp.dot(a, b, preferred_element_type=F32)


_C_AQ, _C_AK, _C_AV = 0, 256, 512
_C_BCQ, _C_BCKV = 768, 1024
_C_CQ, _C_CK, _C_CV = 1152, 1408, 1536
_C_DQ, _C_DK, _C_DV = 1664, 1920, 2048
_C_BKR = 2176
_IN_COLS_PADDED = 2304


def _proj_kernel(x_ref, g_ref, win_ref, qg_ref, wuq_ref, kvg_ref, wuk_ref, wuv_ref,
                 cqg_ref, ckg_ref, tabb_ref, tabd_ref, tabc_ref,
                 aq_ref, ak_ref, av_ref, bqT_ref, bk_ref, bvT_ref,
                 cqT_ref, ck_ref, cvT_ref, dq_ref, dk_ref, dv_ref):
    h = _rms(x_ref[0], g_ref[...]).astype(BF16)

    def proj(c0, width):
        return _bdot(h, win_ref[:, c0:c0 + width])

    def store_heads(ref, z, n_heads):
        for hd in range(n_heads):
            ref[0, hd] = z[:, hd * HEAD_DIM:(hd + 1) * HEAD_DIM].astype(BF16)

    def store_vT_ext(ref, vT, n_heads):
        tm = vT.shape[1]
        pad = V_EXT - HEAD_DIM
        ones_row = (lax.broadcasted_iota(jnp.int32, (pad, tm), 0) == 0).astype(F32).astype(BF16)
        for hd in range(n_heads):
            ref[0, 0, hd * V_EXT:hd * V_EXT + HEAD_DIM, :] = (
                vT[hd * HEAD_DIM:(hd + 1) * HEAD_DIM].astype(BF16))
            ref[0, 0, hd * V_EXT + HEAD_DIM:(hd + 1) * V_EXT, :] = ones_row

    store_heads(aq_ref, proj(_C_AQ, GROUP_WIDTH), GROUP_HEADS)
    store_heads(ak_ref, proj(_C_AK, GROUP_WIDTH), GROUP_HEADS)
    store_heads(av_ref, proj(_C_AV, GROUP_WIDTH), GROUP_HEADS)

    cq = _rms(proj(_C_BCQ, MLA_Q_RANK), qg_ref[...]).astype(BF16)
    qb = _bdot(cq, wuq_ref[...]) * ((MLA_NOPE + MLA_ROPE) ** -0.5 * LOG2E)
    kpe = _rope_lanes(proj(_C_BKR, LANES), tabb_ref, MLA_ROPE // 2)
    ckv = _rms(proj(_C_BCKV, MLA_KV_RANK), kvg_ref[...]).astype(BF16)
    kn = _bdot(ckv, wuk_ref[...])
    for hd in range(GROUP_HEADS):
        blk = slice(hd * LANES, (hd + 1) * LANES)
        qh = _rope_lanes(qb[:, blk], tabb_ref, MLA_ROPE // 2)
        bqT_ref[0, blk, :] = qh.T.astype(BF16)
        bk_ref[0, :, blk] = (kn[:, blk] + kpe).astype(BF16)
    store_vT_ext(bvT_ref, _bdot(ckv, wuv_ref[...]).T, GROUP_HEADS)

    def norm_rope_T(blk, gain_col):
        ms = jnp.mean(blk * blk, axis=0, keepdims=True)
        blk = blk * lax.rsqrt(ms + NORM_EPS) * gain_col
        q = HEAD_DIM // 4
        cr, sr = tabc_ref[0:q], tabc_ref[q:2 * q]
        cc, sc = tabc_ref[2 * q:3 * q], tabc_ref[3 * q:4 * q]
        x1, x2, x3, x4 = blk[0:q], blk[q:2 * q], blk[2 * q:3 * q], blk[3 * q:4 * q]
        return jnp.concatenate([x1 * cr - x2 * sr, x2 * cr + x1 * sr,
                                x3 * cc - x4 * sc, x4 * cc + x3 * sc], axis=0)

    cqT = proj(_C_CQ, GROUP_WIDTH).T
    for hd in range(GROUP_HEADS):
        rows = slice(hd * HEAD_DIM, (hd + 1) * HEAD_DIM)
        cqT_ref[0, rows, :] = norm_rope_T(cqT[rows], cqg_ref[...]).astype(BF16)
    ckT = proj(_C_CK, GROUP_KV_HEADS * HEAD_DIM).T
    ck = jnp.concatenate(
        [norm_rope_T(ckT[hd * HEAD_DIM:(hd + 1) * HEAD_DIM], ckg_ref[...])
         for hd in range(GROUP_KV_HEADS)], axis=0).T
    store_heads(ck_ref, ck, GROUP_KV_HEADS)
    store_vT_ext(cvT_ref, proj(_C_CV, GROUP_KV_HEADS * HEAD_DIM).T, GROUP_KV_HEADS)

    dq = proj(_C_DQ, GROUP_WIDTH)
    dq = jnp.concatenate([_rope_lanes(dq[:, j * LANES:(j + 1) * LANES], tabd_ref, HEAD_DIM // 2)
                          for j in range(GROUP_WIDTH // LANES)], axis=1)
    store_heads(dq_ref, dq, GROUP_HEADS)
    dk = _rope_lanes(proj(_C_DK, GROUP_KV_HEADS * HEAD_DIM), tabd_ref, HEAD_DIM // 2)
    store_heads(dk_ref, dk, GROUP_KV_HEADS)
    store_heads(dv_ref, proj(_C_DV, GROUP_KV_HEADS * HEAD_DIM), GROUP_KV_HEADS)


def _const_spec(shape):
    n = len(shape)
    return pl.BlockSpec(shape, lambda *_: (0,) * n)


def _projection(x, lw, tabs, tm):
    b, s, _ = x.shape
    nt = s // tm
    kvw = GROUP_KV_HEADS * HEAD_DIM
    head_q = jax.ShapeDtypeStruct((b, GROUP_HEADS, s, HEAD_DIM), BF16)
    head_kv = jax.ShapeDtypeStruct((b, GROUP_KV_HEADS, s, HEAD_DIM), BF16)
    out_shape = (
        head_q, head_q, head_q,
        jax.ShapeDtypeStruct((b, GROUP_HEADS * LANES, s), BF16),
        jax.ShapeDtypeStruct((b, s, GROUP_HEADS * LANES), BF16),
        jax.ShapeDtypeStruct((b, nt, GROUP_HEADS * V_EXT, tm), BF16),
        jax.ShapeDtypeStruct((b, GROUP_WIDTH, s), BF16),
        head_kv,
        jax.ShapeDtypeStruct((b, nt, GROUP_KV_HEADS * V_EXT, tm), BF16),
        head_q, head_kv, head_kv,
    )
    hq_spec = pl.BlockSpec((1, GROUP_HEADS, tm, HEAD_DIM), lambda bi, i: (bi, 0, i, 0))
    hkv_spec = pl.BlockSpec((1, GROUP_KV_HEADS, tm, HEAD_DIM), lambda bi, i: (bi, 0, i, 0))
    out_specs = (
        hq_spec, hq_spec, hq_spec,
        pl.BlockSpec((1, GROUP_HEADS * LANES, tm), lambda bi, i: (bi, 0, i)),
        pl.BlockSpec((1, tm, GROUP_HEADS * LANES), lambda bi, i: (bi, i, 0)),
        pl.BlockSpec((1, 1, GROUP_HEADS * V_EXT, tm), lambda bi, i: (bi, i, 0, 0)),
        pl.BlockSpec((1, GROUP_WIDTH, tm), lambda bi, i: (bi, 0, i)),
        hkv_spec,
        pl.BlockSpec((1, 1, GROUP_KV_HEADS * V_EXT, tm), lambda bi, i: (bi, i, 0, 0)),
        hq_spec, hkv_spec, hkv_spec,
    )
    in_specs = [
        pl.BlockSpec((1, tm, D_MODEL), lambda bi, i: (bi, i, 0)),
        _const_spec((1, D_MODEL)),
        _const_spec((D_MODEL, _IN_COLS_PADDED)),
        _const_spec((1, MLA_Q_RANK)),
        _const_spec((MLA_Q_RANK, GROUP_HEADS * LANES)),
        _const_spec((1, MLA_KV_RANK)),
        _const_spec((MLA_KV_RANK, GROUP_HEADS * LANES)),
        _const_spec((MLA_KV_RANK, GROUP_WIDTH)),
        _const_spec((HEAD_DIM, 1)),
        _const_spec((HEAD_DIM, 1)),
        pl.BlockSpec((3, tm, LANES), lambda bi, i: (0, i, 0)),
        pl.BlockSpec((3, tm, LANES), lambda bi, i: (0, i, 0)),
        pl.BlockSpec((HEAD_DIM, tm), lambda bi, i: (0, i)),
    ]
    return pl.pallas_call(
        _proj_kernel,
        grid=(b, nt),
        in_specs=in_specs,
        out_specs=out_specs,
        out_shape=out_shape,
        compiler_params=pltpu.CompilerParams(
            dimension_semantics=("parallel", "parallel"), vmem_limit_bytes=VMEM_LIMIT),
        name="projection",
    )(x, lw["pre_gain"], lw["w_in"], lw["q_gain"], lw["w_uq"], lw["kv_gain"], lw["w_uk"],
      lw["w_uv"], lw["cq_gain"], lw["ck_gain"], tabs["mla"], tabs["full"], tabs["axial"])


def _flash_kernel(qT_ref, k_ref, vT_ref, oT_ref, s_even, s_odd, *, tk, n_chunks, n_streams):
    tq = qT_ref.shape[2] // n_streams
    dv_ext = vT_ref.shape[2]

    def produce(s_ref, c):
        k = k_ref[0, pl.ds(pl.multiple_of(c * tk, tk), tk), :]
        for st in range(n_streams):
            s_ref[st] = _bdot(k, qT_ref[0, :, st * tq:(st + 1) * tq])

    def consume(s_ref, c, carry):
        vT = vT_ref[0, c]
        out = []
        for st, (m, acc) in enumerate(carry):
            s = s_ref[st]
            m_new = jnp.maximum(m, jnp.max(s, axis=0, keepdims=True))
            p = jnp.exp2(s - m_new).astype(BF16)
            acc = jnp.exp2(m - m_new) * acc + _bdot(vT, p)
            out.append((m_new, acc))
        return tuple(out)

    def body(jj, carry):
        produce(s_odd, 2 * jj + 1)
        carry = consume(s_even, 2 * jj, carry)
        produce(s_even, 2 * jj + 2)
        return consume(s_odd, 2 * jj + 1, carry)

    carry = tuple((jnp.full((1, tq), -jnp.inf, F32), jnp.zeros((dv_ext, tq), F32))
                  for _ in range(n_streams))
    produce(s_even, 0)
    carry = lax.fori_loop(0, n_chunks // 2 - 1, body, carry)
    produce(s_odd, n_chunks - 1)
    carry = consume(s_even, n_chunks - 2, carry)
    carry = consume(s_odd, n_chunks - 1, carry)
    for st, (_, acc) in enumerate(carry):
        oT_ref[0, :, st * tq:(st + 1) * tq] = acc[:HEAD_DIM] / acc[HEAD_DIM:HEAD_DIM + 1]


def _flash(qT, k, vT, *, n_heads, n_kv, dk, k_head_major, tq, n_streams):
    b, _, s = qT.shape
    n_chunks, tk = vT.shape[1], vT.shape[3]
    assert n_chunks % 2 == 0 and s % tq == 0
    dv = HEAD_DIM
    rep = n_heads // n_kv
    if k_head_major:
        k_spec = pl.BlockSpec((None, 1, s, dk), lambda bi, h, i: (bi, h // rep, 0, 0))
    else:
        k_spec = pl.BlockSpec((1, s, dk), lambda bi, h, i: (bi, 0, h // rep))
    return pl.pallas_call(
        functools.partial(_flash_kernel, tk=tk, n_chunks=n_chunks, n_streams=n_streams),
        grid=(b, n_heads, s // tq),
        in_specs=[
            pl.BlockSpec((1, dk, tq), lambda bi, h, i: (bi, h, i)),
            k_spec,
            pl.BlockSpec((1, n_chunks, V_EXT, tk), lambda bi, h, i: (bi, 0, h // rep, 0)),
        ],
        out_specs=pl.BlockSpec((1, dv, tq), lambda bi, h, i: (bi, h, i)),
        out_shape=jax.ShapeDtypeStruct((b, n_heads * dv, s), F32),
        scratch_shapes=[pltpu.VMEM((n_streams, tk, tq // n_streams), F32)] * 2,
        compiler_params=pltpu.CompilerParams(
            dimension_semantics=("parallel", "parallel", "parallel"),
            vmem_limit_bytes=VMEM_LIMIT),
        name="dense_attention",
    )(qT, k, vT)


def _na_kernel(q_ref, kp_ref, kc_ref, kn_ref, vp_ref, vc_ref, vn_ref, bias_ref, o_ref,
               k_win, v_win, *, n_rows):
    i = pl.program_id(1)
    blk = NA_ROWS_PER_STEP * GRID_W
    band = NA_WIN_ROWS * GRID_W
    for w, (kr, vr) in enumerate(((kp_ref, vp_ref), (kc_ref, vc_ref), (kn_ref, vn_ref))):
        k_win[:, w * blk:(w + 1) * blk, :] = kr[0]
        v_win[:, w * blk:(w + 1) * blk, :] = vr[0]
    for j in range(NA_ROWS_PER_STEP):
        r = i * NA_ROWS_PER_STEP + j
        rs = jnp.clip(r - NA_WIN_ROWS // 2, 0, n_rows - NA_WIN_ROWS)
        off = pl.multiple_of((rs - (i - 1) * NA_ROWS_PER_STEP) * GRID_W, GRID_W)
        d = r - rs
        for hd in range(GROUP_HEADS):
            q = q_ref[0, hd, j * GRID_W:(j + 1) * GRID_W, :]
            kb = k_win[hd, pl.ds(off, band), :]
            vb = v_win[hd, pl.ds(off, band), :]
            s = lax.dot_general(q, kb, (((1,), (1,)), ((), ())), preferred_element_type=F32)
            s = s + bias_ref[hd, d]
            p = jnp.exp(s - jnp.max(s, axis=-1, keepdims=True))
            o = _bdot(p.astype(BF16), vb) / jnp.sum(p, axis=-1, keepdims=True)
            o_ref[0, j * GRID_W:(j + 1) * GRID_W, hd * HEAD_DIM:(hd + 1) * HEAD_DIM] = o


def _neighbourhood(q, k, v, bias):
    b, nh, s, hd = q.shape
    blk = NA_ROWS_PER_STEP * GRID_W
    nb = s // blk
    n_rows = s // GRID_W
    cur = lambda bi, i: (bi, 0, i, 0)
    prev = lambda bi, i: (bi, 0, jnp.maximum(i - 1, 0), 0)
    nxt = lambda bi, i: (bi, 0, jnp.minimum(i + 1, nb - 1), 0)
    spec = lambda f: pl.BlockSpec((1, nh, blk, hd), f)
    return pl.pallas_call(
        functools.partial(_na_kernel, n_rows=n_rows),
        grid=(b, nb),
        in_specs=[spec(cur), spec(prev), spec(cur), spec(nxt), spec(prev), spec(cur), spec(nxt),
                  _const_spec(bias.shape)],
        out_specs=pl.BlockSpec((1, blk, nh * hd), lambda bi, i: (bi, i, 0)),
        out_shape=jax.ShapeDtypeStruct((b, s, nh * hd), F32),
        scratch_shapes=[pltpu.VMEM((nh, 3 * blk, hd), BF16), pltpu.VMEM((nh, 3 * blk, hd), BF16)],
        compiler_params=pltpu.CompilerParams(
            dimension_semantics=("parallel", "parallel"), vmem_limit_bytes=VMEM_LIMIT),
        name="neighbourhood_attention",
    )(q, k, k, k, v, v, v, bias)


def _na_bias_table(rpb):
    c = np.arange(GRID_W)[:, None]
    kc = np.arange(GRID_W)[None, :]
    cs = np.clip(c - NA_WIN_COLS // 2, 0, GRID_W - NA_WIN_COLS)
    valid = (kc >= cs) & (kc < cs + NA_WIN_COLS)
    col_off = kc - c + (NA_WIN_COLS - 1)
    n_off = 2 * NA_WIN_COLS - 1
    select = (valid[:, :, None] & (col_off[:, :, None] == np.arange(n_off))).astype(np.float32)
    x = jnp.einsum("hro,cko->hrck", rpb.astype(F32), jnp.asarray(select),
                   precision=lax.Precision.HIGHEST)
    x = jnp.where(valid[None, None], x, MASK_VALUE)
    t = jnp.stack([x[:, NA_WIN_ROWS - 1 - d:2 * NA_WIN_ROWS - 1 - d] for d in range(NA_WIN_ROWS)],
                  axis=1)
    t = t.transpose(0, 1, 3, 2, 4)
    return t.reshape(rpb.shape[0], NA_WIN_ROWS, GRID_W, NA_WIN_ROWS * GRID_W)


def _sw_kernel(sink_ref, q_ref, kp_ref, kc_ref, kn_ref, vp_ref, vc_ref, vn_ref, o_ref, *, seq):
    i = pl.program_id(1)
    tq = q_ref.shape[2]
    span = tq + 2 * SW_WINDOW
    t0 = i * tq
    row = lax.broadcasted_iota(jnp.int32, (tq, span), 0)
    col = lax.broadcasted_iota(jnp.int32, (tq, span), 1)
    kpos = col + (t0 - SW_WINDOW)
    rel = col - row
    valid = (rel >= 0) & (rel <= 2 * SW_WINDOW) & (kpos >= 0) & (kpos < seq)
    nt = (((1,), (1,)), ((), ()))
    rep = GROUP_HEADS // GROUP_KV_HEADS
    for hd in range(GROUP_HEADS):
        g = hd // rep
        q = q_ref[0, hd]
        s = jnp.concatenate(
            [lax.dot_general(q, kr[0, g], nt, preferred_element_type=F32)
             for kr in (kp_ref, kc_ref, kn_ref)], axis=1)
        s = jnp.where(valid, s, MASK_VALUE)
        sink = sink_ref[hd]
        m = jnp.maximum(jnp.max(s, axis=-1, keepdims=True), sink)
        p = jnp.exp(s - m)
        denom = jnp.sum(p, axis=-1, keepdims=True) + jnp.exp(sink - m)
        pb = p.astype(BF16)
        o = (_bdot(pb[:, :SW_WINDOW], vp_ref[0, g])
             + _bdot(pb[:, SW_WINDOW:SW_WINDOW + tq], vc_ref[0, g])
             + _bdot(pb[:, SW_WINDOW + tq:], vn_ref[0, g]))
        o_ref[0, :, hd * HEAD_DIM:(hd + 1) * HEAD_DIM] = o / denom


def _sliding_window(q, k, v, sink, tq):
    b, nh, s, hd = q.shape
    nkv = k.shape[1]
    nb = s // tq
    r = tq // SW_WINDOW
    n_small = s // SW_WINDOW
    cur = pl.BlockSpec((1, nkv, tq, hd), lambda bi, i: (bi, 0, i, 0))
    prev = pl.BlockSpec((1, nkv, SW_WINDOW, hd), lambda bi, i: (bi, 0, jnp.maximum(i * r - 1, 0), 0))
    nxt = pl.BlockSpec((1, nkv, SW_WINDOW, hd),
                       lambda bi, i: (bi, 0, jnp.minimum((i + 1) * r, n_small - 1), 0))
    return pl.pallas_call(
        functools.partial(_sw_kernel, seq=s),
        grid=(b, nb),
        in_specs=[pl.BlockSpec(memory_space=pltpu.SMEM),
                  pl.BlockSpec((1, nh, tq, hd), lambda bi, i: (bi, 0, i, 0)),
                  prev, cur, nxt, prev, cur, nxt],
        out_specs=pl.BlockSpec((1, tq, nh * hd), lambda bi, i: (bi, i, 0)),
        out_shape=jax.ShapeDtypeStruct((b, s, nh * hd), F32),
        compiler_params=pltpu.CompilerParams(
            dimension_semantics=("parallel", "parallel"), vmem_limit_bytes=VMEM_LIMIT),
        name="sliding_window_attention",
    )(sink, q, k, k, k, v, v, v)


def _out_kernel(x_ref, oa_ref, obT_ref, ocT_ref, od_ref, w_ref, g_ref, o_ref):
    mixed_in = jnp.concatenate(
        [oa_ref[0].astype(BF16), obT_ref[0].T.astype(BF16), ocT_ref[0].T.astype(BF16),
         od_ref[0].astype(BF16)], axis=1)
    mixed = _bdot(mixed_in, w_ref[...])
    o_ref[0] = x_ref[0] + _rms(mixed, g_ref[...])


def _out_projection(x, o_a, o_bT, o_cT, o_d, w_out, gain, tm):
    b, s, _ = x.shape
    tok = lambda w: pl.BlockSpec((1, tm, w), lambda bi, i: (bi, i, 0))
    feat = pl.BlockSpec((1, GROUP_WIDTH, tm), lambda bi, i: (bi, 0, i))
    return pl.pallas_call(
        _out_kernel,
        grid=(b, s // tm),
        in_specs=[tok(D_MODEL), tok(GROUP_WIDTH), feat, feat, tok(GROUP_WIDTH),
                  _const_spec((D_MODEL, D_MODEL)), _const_spec((1, D_MODEL))],
        out_specs=tok(D_MODEL),
        out_shape=jax.ShapeDtypeStruct(x.shape, F32),
        compiler_params=pltpu.CompilerParams(
            dimension_semantics=("parallel", "parallel"), vmem_limit_bytes=VMEM_LIMIT),
        name="out_projection",
    )(x, o_a, o_bT, o_cT, o_d, w_out, gain)


FFN_HALO = 8


def _ffn_kernel(x_ref, xp_ref, xn_ref, g_ref, wg_ref, wv_ref, cwg_ref, cwv_ref, cbg_ref, cbv_ref,
                wd_ref, pg_ref, o_ref, h_scr, acc_scr, *, n_chunks):
    i = pl.program_id(1)
    n_tiles = pl.num_programs(1)
    tm = x_ref.shape[1]
    ext = tm + 2 * FFN_HALO
    g = g_ref[...]
    hp = _rms(xp_ref[0], g) * (i > 0).astype(F32)
    hn = _rms(xn_ref[0], g) * (i < n_tiles - 1).astype(F32)
    h_scr[...] = jnp.concatenate([hp, _rms(x_ref[0], g), hn], axis=0).astype(BF16)
    acc_scr[...] = jnp.zeros_like(acc_scr)

    def conv(u, cw, cb):
        u_prev = pltpu.roll(u, 1, 0)
        u_next = pltpu.roll(u, ext - 1, 0)
        c = cb + u_prev * cw[0:1] + u * cw[1:2] + u_next * cw[2:3]
        return c[FFN_HALO:FFN_HALO + tm]

    def body(c, carry):
        hh = h_scr[...]
        gate = conv(_bdot(hh, wg_ref[c]), cwg_ref[c], cbg_ref[c])
        val = conv(_bdot(hh, wv_ref[c]), cwv_ref[c], cbv_ref[c])
        act = jax.nn.gelu(gate, approximate=True) * val
        acc_scr[...] += _bdot(act.astype(BF16), wd_ref[c])
        return carry

    lax.fori_loop(0, n_chunks, body, 0)
    o_ref[0] = x_ref[0] + _rms(acc_scr[...], pg_ref[...])


def _ffn(x, lw, tm):
    b, s, _ = x.shape
    n_chunks, _, fc = lw["w_gate"].shape
    r = tm // FFN_HALO
    n_halo = s // FFN_HALO
    tile = pl.BlockSpec((1, tm, D_MODEL), lambda bi, i: (bi, i, 0))
    prev = pl.BlockSpec((1, FFN_HALO, D_MODEL), lambda bi, i: (bi, jnp.maximum(i * r - 1, 0), 0))
    nxt = pl.BlockSpec((1, FFN_HALO, D_MODEL),
                       lambda bi, i: (bi, jnp.minimum((i + 1) * r, n_halo - 1), 0))
    return pl.pallas_call(
        functools.partial(_ffn_kernel, n_chunks=n_chunks),
        grid=(b, s // tm),
        in_specs=[tile, prev, nxt, _const_spec((1, D_MODEL)),
                  _const_spec((n_chunks, D_MODEL, fc)), _const_spec((n_chunks, D_MODEL, fc)),
                  _const_spec((n_chunks, 3, fc)), _const_spec((n_chunks, 3, fc)),
                  _const_spec((n_chunks, 1, fc)), _const_spec((n_chunks, 1, fc)),
                  _const_spec((n_chunks, fc, D_MODEL)), _const_spec((1, D_MODEL))],
        out_specs=tile,
        out_shape=jax.ShapeDtypeStruct(x.shape, F32),
        scratch_shapes=[pltpu.VMEM((tm + 2 * FFN_HALO, D_MODEL), BF16),
                        pltpu.VMEM((tm, D_MODEL), F32)],
        compiler_params=pltpu.CompilerParams(
            dimension_semantics=("parallel", "parallel"), vmem_limit_bytes=VMEM_LIMIT),
        name="conv_mlp",
    )(x, x, x, lw["ffn_pre_gain"], lw["w_gate"], lw["w_val"], lw["cw_gate"], lw["cw_val"],
      lw["cb_gate"], lw["cb_val"], lw["w_down"], lw["ffn_post_gain"])


def _rope_tables(s):
    t = jnp.arange(s)

    def angles(pos, dim):
        inv = ROPE_THETA ** (-jnp.arange(0, dim, 2, dtype=F32) / dim)
        return pos.astype(F32)[:, None] * inv[None, :]

    def lane_table(ang, lead, trail, reps):
        half = ang.shape[1]
        cos, sin, zero = jnp.cos(ang), jnp.sin(ang), jnp.zeros_like(ang)
        one = lambda n: jnp.ones((s, n), F32)
        nul = lambda n: jnp.zeros((s, n), F32)
        c = jnp.concatenate([one(lead)] + [cos, cos] * reps + [one(trail)], axis=1)
        lo = jnp.concatenate([nul(lead)] + [-sin, zero] * reps + [nul(trail)], axis=1)
        hi = jnp.concatenate([nul(lead)] + [zero, sin] * reps + [nul(trail)], axis=1)
        assert c.shape[1] == LANES and 2 * half * reps + lead + trail == LANES
        return jnp.stack([c, lo, hi])

    ang_row = angles(t // GRID_W, HEAD_DIM // 2)
    ang_col = angles(t % GRID_W, HEAD_DIM // 2)
    return {
        "mla": lane_table(angles(t, MLA_ROPE), MLA_NOPE, LANES - MLA_NOPE - MLA_ROPE, 1),
        "full": lane_table(angles(t, HEAD_DIM), 0, 0, LANES // HEAD_DIM),
        "axial": jnp.concatenate([jnp.cos(ang_row), jnp.sin(ang_row),
                                  jnp.cos(ang_col), jnp.sin(ang_col)], axis=1).T,
    }


def _layer_weights(l, mix_pre_gain, w_in, na_rpb, mla_q_gain, mla_w_uq, mla_kv_gain, mla_w_ukv,
                   ax_q_gain, ax_k_gain, sw_sink, w_out, mix_post_gain, ffn_pre_gain, w_up,
                   conv_w, conv_b, w_down, ffn_post_gain):
    gw, kvw = GROUP_WIDTH, GROUP_KV_HEADS * HEAD_DIM
    sizes = (gw, gw, gw, MLA_Q_RANK, MLA_KV_RANK, MLA_ROPE, gw, kvw, kvw, gw, kvw, kvw)
    bounds = np.cumsum((0,) + sizes)
    (a_q, a_k, a_v, b_cq, b_ckv, b_kr, c_q, c_k, c_v, d_q, d_k, d_v) = [
        w_in[l][:, bounds[j]:bounds[j + 1]] for j in range(len(sizes))]
    scale = HEAD_DIM ** -0.5
    zeros = lambda n: jnp.zeros((D_MODEL, n), F32)
    kr_block = jnp.concatenate([zeros(MLA_NOPE), b_kr, zeros(LANES - MLA_NOPE - MLA_ROPE)], axis=1)
    w_in_r = jnp.concatenate([a_q * scale, a_k, a_v, b_cq, b_ckv, c_q, c_k, c_v,
                              d_q * scale, d_k, d_v, kr_block], axis=1)
    assert w_in_r.shape[1] == _IN_COLS_PADDED

    uq = mla_w_uq[l].reshape(MLA_Q_RANK, GROUP_HEADS, MLA_NOPE + MLA_ROPE)
    uq = jnp.pad(uq, ((0, 0), (0, 0), (0, LANES - MLA_NOPE - MLA_ROPE)))
    ukv = mla_w_ukv[l].reshape(MLA_KV_RANK, GROUP_HEADS, MLA_NOPE + HEAD_DIM)
    uk = jnp.pad(ukv[:, :, :MLA_NOPE], ((0, 0), (0, 0), (0, LANES - MLA_NOPE)))
    uv = ukv[:, :, MLA_NOPE:]

    n_chunks = D_FF // FF_CHUNK
    chunk_cols = lambda w: w.reshape(w.shape[0], n_chunks, FF_CHUNK).transpose(1, 0, 2)
    row = lambda v: v[None, :].astype(F32)
    return {
        "pre_gain": row(mix_pre_gain[l]),
        "w_in": w_in_r.astype(BF16),
        "q_gain": row(mla_q_gain[l]),
        "w_uq": uq.reshape(MLA_Q_RANK, GROUP_HEADS * LANES).astype(BF16),
        "kv_gain": row(mla_kv_gain[l]),
        "w_uk": uk.reshape(MLA_KV_RANK, GROUP_HEADS * LANES).astype(BF16),
        "w_uv": uv.reshape(MLA_KV_RANK, GROUP_WIDTH).astype(BF16),
        "cq_gain": (ax_q_gain[l] * (scale * LOG2E))[:, None].astype(F32),
        "ck_gain": ax_k_gain[l][:, None].astype(F32),
        "na_bias": _na_bias_table(na_rpb[l]),
        "sink": sw_sink[l].astype(F32),
        "w_out": w_out[l].astype(BF16),
        "post_gain": row(mix_post_gain[l]),
        "ffn_pre_gain": row(ffn_pre_gain[l]),
        "w_gate": chunk_cols(w_up[l][:, :D_FF]).astype(BF16),
        "w_val": chunk_cols(w_up[l][:, D_FF:]).astype(BF16),
        "cw_gate": chunk_cols(conv_w[l][:, :D_FF]),
        "cw_val": chunk_cols(conv_w[l][:, D_FF:]),
        "cb_gate": chunk_cols(conv_b[l][None, :D_FF]),
        "cb_val": chunk_cols(conv_b[l][None, D_FF:]),
        "w_down": w_down[l].reshape(n_chunks, FF_CHUNK, D_MODEL).astype(BF16),
        "ffn_post_gain": row(ffn_post_gain[l]),
    }


def kernel(x, mix_pre_gain, w_in, na_rpb, mla_q_gain, mla_w_uq, mla_kv_gain, mla_w_ukv, ax_q_gain,
           ax_k_gain, sw_sink, w_out, mix_post_gain, ffn_pre_gain, w_up, conv_w, conv_b, w_down,
           ffn_post_gain):
    b, s, d = x.shape
    assert d == D_MODEL and s % max(TOKEN_TILE, FLASH_TQ, NA_ROWS_PER_STEP * GRID_W, SW_TQ) == 0
    assert s // GRID_W >= NA_WIN_ROWS
    params = (mix_pre_gain, w_in, na_rpb, mla_q_gain, mla_w_uq, mla_kv_gain, mla_w_ukv, ax_q_gain,
              ax_k_gain, sw_sink, w_out, mix_post_gain, ffn_pre_gain, w_up, conv_w, conv_b, w_down,
              ffn_post_gain)
    tabs = _rope_tables(s)
    for l in range(w_in.shape[0]):
        lw = _layer_weights(l, *params)
        (a_q, a_k, a_v, b_qT, b_k, b_vT, c_qT, c_k, c_vT, d_q, d_k, d_v) = _projection(
            x, lw, tabs, TOKEN_TILE)
        o_a = _neighbourhood(a_q, a_k, a_v, lw["na_bias"])
        o_bT = _flash(b_qT, b_k, b_vT, n_heads=GROUP_HEADS, n_kv=GROUP_HEADS, dk=LANES,
                      k_head_major=False, tq=FLASH_TQ * FLASH_STREAMS, n_streams=FLASH_STREAMS)
        o_cT = _flash(c_qT, c_k, c_vT, n_heads=GROUP_HEADS, n_kv=GROUP_KV_HEADS, dk=HEAD_DIM,
                      k_head_major=True, tq=FLASH_TQ * FLASH_STREAMS, n_streams=FLASH_STREAMS)
        o_d = _sliding_window(d_q, d_k, d_v, lw["sink"], SW_TQ)
        x = _out_projection(x, o_a, o_bT, o_cT, o_d, lw["w_out"], lw["post_gain"], TOKEN_TILE)
        x = _ffn(x, lw, TOKEN_TILE)
    return x
```

```python
import functools
import math

import numpy as np
import jax
import jax.numpy as jnp
from jax import lax
from jax.experimental import pallas as pl
from jax.experimental.pallas import tpu as pltpu

F32 = jnp.float32
BF16 = jnp.bfloat16

D_MODEL = 1024
GRID_W = 64
HEAD_DIM = 64
GROUP_HEADS = 4
GROUP_KV_HEADS = 2
GROUP_WIDTH = GROUP_HEADS * HEAD_DIM
ROPE_THETA = 10000.0
NORM_EPS = 1e-6
MASK_VALUE = -1e30
NA_WIN_ROWS = 8
NA_WIN_COLS = 16
MLA_Q_RANK = 256
MLA_KV_RANK = 128
MLA_NOPE = 64
MLA_ROPE = 32
SW_WINDOW = 128
D_FF = 2816
LOG2E = math.log2(math.e)

LANES = 128
VMEM_LIMIT = 56 * 1024 * 1024

TOKEN_TILE = 512
FLASH_TQ = 512
FLASH_STREAMS = 2
V_EXT = HEAD_DIM + 16
NA_ROWS_PER_STEP = 8
SW_TQ = 256
FF_CHUNK = 256


def _rms(x, gain):
    return x * lax.rsqrt(jnp.mean(x * x, axis=-1, keepdims=True) + NORM_EPS) * gain


def _rope_lanes(x, tab_ref, half):
    w = x.shape[-1]
    return (x * tab_ref[0] + pltpu.roll(x, w - half, 1) * tab_ref[1]
            + pltpu.roll(x, half, 1) * tab_ref[2])


def _bdot(a, b):
    return jnp.dot(a, b, preferred_element_type=F32)


_C_AQ, _C_AK, _C_AV = 0, 256, 512
_C_BCQ, _C_BCKV, _C_BKR = 768, 1024, 1152
_C_CQ, _C_CK, _C_CV = 1280, 1536, 1664
_C_DQ, _C_DK, _C_DV = 1792, 2048, 2176
_IN_COLS_PADDED = 2304
_PAIR = 2 * LANES


def _proj_kernel(x_ref, g_ref, win_ref, qg_ref, wuq_ref, kvg_ref, wuk_ref, wuv_ref,
                 cqg_ref, ckg_ref, tabb_ref, tabd_ref, tabc_ref,
                 aq_ref, ak_ref, av_ref, bqT_ref, bk_ref, bvT_ref,
                 cqT_ref, ck_ref, cvT_ref, dq_ref, dk_ref, dv_ref):
    h = _rms(x_ref[0], g_ref[...]).astype(BF16)

    def proj(c0, width):
        return _bdot(h, win_ref[:, c0:c0 + width])

    def store_heads(ref, z, n_heads):
        for hd in range(n_heads):
            ref[0, hd] = z[:, hd * HEAD_DIM:(hd + 1) * HEAD_DIM].astype(BF16)

    def store_vT_ext(ref, vT, n_heads):
        tm = vT.shape[1]
        pad = V_EXT - HEAD_DIM
        ones_row = (lax.broadcasted_iota(jnp.int32, (pad, tm), 0) == 0).astype(F32).astype(BF16)
        for hd in range(n_heads):
            ref[0, 0, hd * V_EXT:hd * V_EXT + HEAD_DIM, :] = (
                vT[hd * HEAD_DIM:(hd + 1) * HEAD_DIM].astype(BF16))
            ref[0, 0, hd * V_EXT + HEAD_DIM:(hd + 1) * V_EXT, :] = ones_row

    store_heads(aq_ref, proj(_C_AQ, GROUP_WIDTH), GROUP_HEADS)
    store_heads(ak_ref, proj(_C_AK, GROUP_WIDTH), GROUP_HEADS)
    store_heads(av_ref, proj(_C_AV, GROUP_WIDTH), GROUP_HEADS)

    cq = _rms(proj(_C_BCQ, MLA_Q_RANK), qg_ref[...]).astype(BF16)
    qb = _bdot(cq, wuq_ref[...]) * ((MLA_NOPE + MLA_ROPE) ** -0.5 * LOG2E)
    assert (_C_BKR, _C_CV, _C_DV) == (_C_BCKV + LANES, _C_CK + LANES, _C_DK + LANES)
    b_pair = proj(_C_BCKV, _PAIR)
    kpe = _rope_lanes(b_pair[:, LANES:], tabb_ref, MLA_ROPE // 2)
    ckv = _rms(b_pair[:, :LANES], kvg_ref[...]).astype(BF16)
    kn = _bdot(ckv, wuk_ref[...])
    for hd in range(GROUP_HEADS):
        blk = slice(hd * LANES, (hd + 1) * LANES)
        qh = _rope_lanes(qb[:, blk], tabb_ref, MLA_ROPE // 2)
        bqT_ref[0, blk, :] = qh.T.astype(BF16)
        bk_ref[0, :, blk] = (kn[:, blk] + kpe).astype(BF16)
    store_vT_ext(bvT_ref, _bdot(ckv, wuv_ref[...]).T, GROUP_HEADS)

    def norm_rope_T(blk, gain_col):
        ms = jnp.mean(blk * blk, axis=0, keepdims=True)
        blk = blk * lax.rsqrt(ms + NORM_EPS) * gain_col
        q = HEAD_DIM // 4
        cr, sr = tabc_ref[0:q], tabc_ref[q:2 * q]
        cc, sc = tabc_ref[2 * q:3 * q], tabc_ref[3 * q:4 * q]
        x1, x2, x3, x4 = blk[0:q], blk[q:2 * q], blk[2 * q:3 * q], blk[3 * q:4 * q]
        return jnp.concatenate([x1 * cr - x2 * sr, x2 * cr + x1 * sr,
                                x3 * cc - x4 * sc, x4 * cc + x3 * sc], axis=0)

    cqT = proj(_C_CQ, GROUP_WIDTH).T
    for hd in range(GROUP_HEADS):
        rows = slice(hd * HEAD_DIM, (hd + 1) * HEAD_DIM)
        cqT_ref[0, rows, :] = norm_rope_T(cqT[rows], cqg_ref[...]).astype(BF16)
    c_pair = proj(_C_CK, _PAIR)
    ckT = c_pair[:, :LANES].T
    ck = jnp.concatenate(
        [norm_rope_T(ckT[hd * HEAD_DIM:(hd + 1) * HEAD_DIM], ckg_ref[...])
         for hd in range(GROUP_KV_HEADS)], axis=0).T
    store_heads(ck_ref, ck, GROUP_KV_HEADS)
    store_vT_ext(cvT_ref, c_pair[:, LANES:].T, GROUP_KV_HEADS)

    dq = proj(_C_DQ, GROUP_WIDTH)
    dq = jnp.concatenate([_rope_lanes(dq[:, j * LANES:(j + 1) * LANES], tabd_ref, HEAD_DIM // 2)
                          for j in range(GROUP_WIDTH // LANES)], axis=1)
    store_heads(dq_ref, dq, GROUP_HEADS)
    d_pair = proj(_C_DK, _PAIR)
    dk = _rope_lanes(d_pair[:, :LANES], tabd_ref, HEAD_DIM // 2)
    store_heads(dk_ref, dk, GROUP_KV_HEADS)
    store_heads(dv_ref, d_pair[:, LANES:], GROUP_KV_HEADS)


def _const_spec(shape):
    n = len(shape)
    return pl.BlockSpec(shape, lambda *_: (0,) * n)


def _projection(x, lw, tabs, tm):
    b, s, _ = x.shape
    nt = s // tm
    kvw = GROUP_KV_HEADS * HEAD_DIM
    head_q = jax.ShapeDtypeStruct((b, GROUP_HEADS, s, HEAD_DIM), BF16)
    head_kv = jax.ShapeDtypeStruct((b, GROUP_KV_HEADS, s, HEAD_DIM), BF16)
    out_shape = (
        head_q, head_q, head_q,
        jax.ShapeDtypeStruct((b, GROUP_HEADS * LANES, s), BF16),
        jax.ShapeDtypeStruct((b, s, GROUP_HEADS * LANES), BF16),
        jax.ShapeDtypeStruct((b, nt, GROUP_HEADS * V_EXT, tm), BF16),
        jax.ShapeDtypeStruct((b, GROUP_WIDTH, s), BF16),
        head_kv,
        jax.ShapeDtypeStruct((b, nt, GROUP_KV_HEADS * V_EXT, tm), BF16),
        head_q, head_kv, head_kv,
    )
    hq_spec = pl.BlockSpec((1, GROUP_HEADS, tm, HEAD_DIM), lambda bi, i: (bi, 0, i, 0))
    hkv_spec = pl.BlockSpec((1, GROUP_KV_HEADS, tm, HEAD_DIM), lambda bi, i: (bi, 0, i, 0))
    out_specs = (
        hq_spec, hq_spec, hq_spec,
        pl.BlockSpec((1, GROUP_HEADS * LANES, tm), lambda bi, i: (bi, 0, i)),
        pl.BlockSpec((1, tm, GROUP_HEADS * LANES), lambda bi, i: (bi, i, 0)),
        pl.BlockSpec((1, 1, GROUP_HEADS * V_EXT, tm), lambda bi, i: (bi, i, 0, 0)),
        pl.BlockSpec((1, GROUP_WIDTH, tm), lambda bi, i: (bi, 0, i)),
        hkv_spec,
        pl.BlockSpec((1, 1, GROUP_KV_HEADS * V_EXT, tm), lambda bi, i: (bi, i, 0, 0)),
        hq_spec, hkv_spec, hkv_spec,
    )
    in_specs = [
        pl.BlockSpec((1, tm, D_MODEL), lambda bi, i: (bi, i, 0)),
        _const_spec((1, D_MODEL)),
        _const_spec((D_MODEL, _IN_COLS_PADDED)),
        _const_spec((1, MLA_Q_RANK)),
        _const_spec((MLA_Q_RANK, GROUP_HEADS * LANES)),
        _const_spec((1, MLA_KV_RANK)),
        _const_spec((MLA_KV_RANK, GROUP_HEADS * LANES)),
        _const_spec((MLA_KV_RANK, GROUP_WIDTH)),
        _const_spec((HEAD_DIM, 1)),
        _const_spec((HEAD_DIM, 1)),
        pl.BlockSpec((3, tm, LANES), lambda bi, i: (0, i, 0)),
        pl.BlockSpec((3, tm, LANES), lambda bi, i: (0, i, 0)),
        pl.BlockSpec((HEAD_DIM, tm), lambda bi, i: (0, i)),
    ]
    return pl.pallas_call(
        _proj_kernel,
        grid=(b, nt),
        in_specs=in_specs,
        out_specs=out_specs,
        out_shape=out_shape,
        compiler_params=pltpu.CompilerParams(
            dimension_semantics=("parallel", "parallel"), vmem_limit_bytes=VMEM_LIMIT),
        name="projection",
    )(x, lw["pre_gain"], lw["w_in"], lw["q_gain"], lw["w_uq"], lw["kv_gain"], lw["w_uk"],
      lw["w_uv"], lw["cq_gain"], lw["ck_gain"], tabs["mla"], tabs["full"], tabs["axial"])


def _flash_kernel(qT_ref, k_ref, vT_ref, oT_ref, s_even, s_odd, *, tk, n_chunks, n_streams):
    tq = qT_ref.shape[2] // n_streams
    dv_ext = vT_ref.shape[2]

    def produce(s_ref, c):
        k = k_ref[0, pl.ds(pl.multiple_of(c * tk, tk), tk), :]
        for st in range(n_streams):
            s_ref[st] = _bdot(k, qT_ref[0, :, st * tq:(st + 1) * tq])

    def consume(s_ref, c, carry):
        vT = vT_ref[0, c]
        out = []
        for st, (m, acc) in enumerate(carry):
            s = s_ref[st]
            m_new = jnp.maximum(m, jnp.max(s, axis=0, keepdims=True))
            p = jnp.exp2(s - m_new).astype(BF16)
            acc = jnp.exp2(m - m_new) * acc + _bdot(vT, p)
            out.append((m_new, acc))
        return tuple(out)

    def body(jj, carry):
        produce(s_odd, 2 * jj + 1)
        carry = consume(s_even, 2 * jj, carry)
        produce(s_even, 2 * jj + 2)
        return consume(s_odd, 2 * jj + 1, carry)

    carry = tuple((jnp.full((1, tq), -jnp.inf, F32), jnp.zeros((dv_ext, tq), F32))
                  for _ in range(n_streams))
    produce(s_even, 0)
    carry = lax.fori_loop(0, n_chunks // 2 - 1, body, carry)
    produce(s_odd, n_chunks - 1)
    carry = consume(s_even, n_chunks - 2, carry)
    carry = consume(s_odd, n_chunks - 1, carry)
    for st, (_, acc) in enumerate(carry):
        oT_ref[0, :, st * tq:(st + 1) * tq] = acc[:HEAD_DIM] / acc[HEAD_DIM:HEAD_DIM + 1]


def _flash(qT, k, vT, *, n_heads, n_kv, dk, k_head_major, tq, n_streams):
    b, _, s = qT.shape
    n_chunks, tk = vT.shape[1], vT.shape[3]
    assert n_chunks % 2 == 0 and s % tq == 0
    dv = HEAD_DIM
    rep = n_heads // n_kv
    if k_head_major:
        k_spec = pl.BlockSpec((None, 1, s, dk), lambda bi, h, i: (bi, h // rep, 0, 0))
    else:
        k_spec = pl.BlockSpec((1, s, dk), lambda bi, h, i: (bi, 0, h // rep))
    return pl.pallas_call(
        functools.partial(_flash_kernel, tk=tk, n_chunks=n_chunks, n_streams=n_streams),
        grid=(b, n_heads, s // tq),
        in_specs=[
            pl.BlockSpec((1, dk, tq), lambda bi, h, i: (bi, h, i)),
            k_spec,
            pl.BlockSpec((1, n_chunks, V_EXT, tk), lambda bi, h, i: (bi, 0, h // rep, 0)),
        ],
        out_specs=pl.BlockSpec((1, dv, tq), lambda bi, h, i: (bi, h, i)),
        out_shape=jax.ShapeDtypeStruct((b, n_heads * dv, s), F32),
        scratch_shapes=[pltpu.VMEM((n_streams, tk, tq // n_streams), F32)] * 2,
        compiler_params=pltpu.CompilerParams(
            dimension_semantics=("parallel", "parallel", "parallel"),
            vmem_limit_bytes=VMEM_LIMIT),
        name="dense_attention",
    )(qT, k, vT)


def _na_kernel(q_ref, kp_ref, kc_ref, kn_ref, vp_ref, vc_ref, vn_ref, bias_ref, o_ref,
               k_win, v_win, *, n_rows):
    i = pl.program_id(1)
    blk = NA_ROWS_PER_STEP * GRID_W
    band = NA_WIN_ROWS * GRID_W
    for w, (kr, vr) in enumerate(((kp_ref, vp_ref), (kc_ref, vc_ref), (kn_ref, vn_ref))):
        k_win[:, w * blk:(w + 1) * blk, :] = kr[0]
        v_win[:, w * blk:(w + 1) * blk, :] = vr[0]
    for j in range(NA_ROWS_PER_STEP):
        r = i * NA_ROWS_PER_STEP + j
        rs = jnp.clip(r - NA_WIN_ROWS // 2, 0, n_rows - NA_WIN_ROWS)
        off = pl.multiple_of((rs - (i - 1) * NA_ROWS_PER_STEP) * GRID_W, GRID_W)
        d = r - rs
        heads = range(GROUP_HEADS)
        ss = [lax.dot_general(q_ref[0, hd, j * GRID_W:(j + 1) * GRID_W, :],
                              k_win[hd, pl.ds(off, band), :],
                              (((1,), (1,)), ((), ())), preferred_element_type=F32)
              + bias_ref[hd, d] for hd in heads]
        ps = [jnp.exp(s - jnp.max(s, axis=-1, keepdims=True)) for s in ss]
        os = [_bdot(ps[hd].astype(BF16), v_win[hd, pl.ds(off, band), :])
              / jnp.sum(ps[hd], axis=-1, keepdims=True) for hd in heads]
        o_ref[0, j * GRID_W:(j + 1) * GRID_W, :] = jnp.concatenate(os, axis=1)


def _neighbourhood(q, k, v, bias):
    b, nh, s, hd = q.shape
    blk = NA_ROWS_PER_STEP * GRID_W
    nb = s // blk
    n_rows = s // GRID_W
    cur = lambda bi, i: (bi, 0, i, 0)
    prev = lambda bi, i: (bi, 0, jnp.maximum(i - 1, 0), 0)
    nxt = lambda bi, i: (bi, 0, jnp.minimum(i + 1, nb - 1), 0)
    spec = lambda f: pl.BlockSpec((1, nh, blk, hd), f)
    return pl.pallas_call(
        functools.partial(_na_kernel, n_rows=n_rows),
        grid=(b, nb),
        in_specs=[spec(cur), spec(prev), spec(cur), spec(nxt), spec(prev), spec(cur), spec(nxt),
                  _const_spec(bias.shape)],
        out_specs=pl.BlockSpec((1, blk, nh * hd), lambda bi, i: (bi, i, 0)),
        out_shape=jax.ShapeDtypeStruct((b, s, nh * hd), F32),
        scratch_shapes=[pltpu.VMEM((nh, 3 * blk, hd), BF16), pltpu.VMEM((nh, 3 * blk, hd), BF16)],
        compiler_params=pltpu.CompilerParams(
            dimension_semantics=("parallel", "parallel"), vmem_limit_bytes=VMEM_LIMIT),
        name="neighbourhood_attention",
    )(q, k, k, k, v, v, v, bias)


def _na_bias_table(rpb):
    c = np.arange(GRID_W)[:, None]
    kc = np.arange(GRID_W)[None, :]
    cs = np.clip(c - NA_WIN_COLS // 2, 0, GRID_W - NA_WIN_COLS)
    valid = (kc >= cs) & (kc < cs + NA_WIN_COLS)
    col_off = kc - c + (NA_WIN_COLS - 1)
    n_off = 2 * NA_WIN_COLS - 1
    select = (valid[:, :, None] & (col_off[:, :, None] == np.arange(n_off))).astype(np.float32)
    x = jnp.einsum("hro,cko->hrck", rpb.astype(F32), jnp.asarray(select),
                   precision=lax.Precision.HIGHEST)
    x = jnp.where(valid[None, None], x, MASK_VALUE)
    t = jnp.stack([x[:, NA_WIN_ROWS - 1 - d:2 * NA_WIN_ROWS - 1 - d] for d in range(NA_WIN_ROWS)],
                  axis=1)
    t = t.transpose(0, 1, 3, 2, 4)
    return t.reshape(rpb.shape[0], NA_WIN_ROWS, GRID_W, NA_WIN_ROWS * GRID_W)


def _sw_kernel(sink_ref, q_ref, kp_ref, kc_ref, kn_ref, vp_ref, vc_ref, vn_ref, o_ref, *, seq):
    i = pl.program_id(1)
    tq = q_ref.shape[2]
    span = tq + 2 * SW_WINDOW
    t0 = i * tq
    row = lax.broadcasted_iota(jnp.int32, (tq, span), 0)
    col = lax.broadcasted_iota(jnp.int32, (tq, span), 1)
    kpos = col + (t0 - SW_WINDOW)
    rel = col - row
    valid = (rel >= 0) & (rel <= 2 * SW_WINDOW) & (kpos >= 0) & (kpos < seq)
    nt = (((1,), (1,)), ((), ()))
    rep = GROUP_HEADS // GROUP_KV_HEADS
    for hd in range(GROUP_HEADS):
        g = hd // rep
        q = q_ref[0, hd]
        s = jnp.concatenate(
            [lax.dot_general(q, kr[0, g], nt, preferred_element_type=F32)
             for kr in (kp_ref, kc_ref, kn_ref)], axis=1)
        s = jnp.where(valid, s, MASK_VALUE)
        sink = sink_ref[hd]
        m = jnp.maximum(jnp.max(s, axis=-1, keepdims=True), sink)
        p = jnp.exp(s - m)
        denom = jnp.sum(p, axis=-1, keepdims=True) + jnp.exp(sink - m)
        pb = p.astype(BF16)
        o = (_bdot(pb[:, :SW_WINDOW], vp_ref[0, g])
             + _bdot(pb[:, SW_WINDOW:SW_WINDOW + tq], vc_ref[0, g])
             + _bdot(pb[:, SW_WINDOW + tq:], vn_ref[0, g]))
        o_ref[0, :, hd * HEAD_DIM:(hd + 1) * HEAD_DIM] = o / denom


def _sliding_window(q, k, v, sink, tq):
    b, nh, s, hd = q.shape
    nkv = k.shape[1]
    nb = s // tq
    r = tq // SW_WINDOW
    n_small = s // SW_WINDOW
    cur = pl.BlockSpec((1, nkv, tq, hd), lambda bi, i: (bi, 0, i, 0))
    prev = pl.BlockSpec((1, nkv, SW_WINDOW, hd), lambda bi, i: (bi, 0, jnp.maximum(i * r - 1, 0), 0))
    nxt = pl.BlockSpec((1, nkv, SW_WINDOW, hd),
                       lambda bi, i: (bi, 0, jnp.minimum((i + 1) * r, n_small - 1), 0))
    return pl.pallas_call(
        functools.partial(_sw_kernel, seq=s),
        grid=(b, nb),
        in_specs=[pl.BlockSpec(memory_space=pltpu.SMEM),
                  pl.BlockSpec((1, nh, tq, hd), lambda bi, i: (bi, 0, i, 0)),
                  prev, cur, nxt, prev, cur, nxt],
        out_specs=pl.BlockSpec((1, tq, nh * hd), lambda bi, i: (bi, i, 0)),
        out_shape=jax.ShapeDtypeStruct((b, s, nh * hd), F32),
        compiler_params=pltpu.CompilerParams(
            dimension_semantics=("parallel", "parallel"), vmem_limit_bytes=VMEM_LIMIT),
        name="sliding_window_attention",
    )(sink, q, k, k, k, v, v, v)


def _out_kernel(x_ref, oa_ref, obT_ref, ocT_ref, od_ref, w_ref, g_ref, o_ref):
    mixed_in = jnp.concatenate(
        [oa_ref[0].astype(BF16), obT_ref[0].T.astype(BF16), ocT_ref[0].T.astype(BF16),
         od_ref[0].astype(BF16)], axis=1)
    mixed = _bdot(mixed_in, w_ref[...])
    o_ref[0] = x_ref[0] + _rms(mixed, g_ref[...])


def _out_projection(x, o_a, o_bT, o_cT, o_d, w_out, gain, tm):
    b, s, _ = x.shape
    tok = lambda w: pl.BlockSpec((1, tm, w), lambda bi, i: (bi, i, 0))
    feat = pl.BlockSpec((1, GROUP_WIDTH, tm), lambda bi, i: (bi, 0, i))
    return pl.pallas_call(
        _out_kernel,
        grid=(b, s // tm),
        in_specs=[tok(D_MODEL), tok(GROUP_WIDTH), feat, feat, tok(GROUP_WIDTH),
                  _const_spec((D_MODEL, D_MODEL)), _const_spec((1, D_MODEL))],
        out_specs=tok(D_MODEL),
        out_shape=jax.ShapeDtypeStruct(x.shape, F32),
        compiler_params=pltpu.CompilerParams(
            dimension_semantics=("parallel", "parallel"), vmem_limit_bytes=VMEM_LIMIT),
        name="out_projection",
    )(x, o_a, o_bT, o_cT, o_d, w_out, gain)


FFN_HALO = 8


def _ffn_kernel(x_ref, xp_ref, xn_ref, g_ref, wg_ref, wv_ref, cwg_ref, cwv_ref, cbg_ref, cbv_ref,
                wd_ref, pg_ref, o_ref, h_scr, acc_scr, u_even, u_odd, *, n_chunks):
    i = pl.program_id(1)
    n_tiles = pl.num_programs(1)
    tm = x_ref.shape[1]
    ext = tm + 2 * FFN_HALO
    g = g_ref[...]
    hp = _rms(xp_ref[0], g) * (i > 0).astype(F32)
    hn = _rms(xn_ref[0], g) * (i < n_tiles - 1).astype(F32)
    h_scr[...] = jnp.concatenate([hp, _rms(x_ref[0], g), hn], axis=0).astype(BF16)
    acc_scr[...] = jnp.zeros_like(acc_scr)

    def produce(u_ref, c):
        hh = h_scr[...]
        u_ref[0] = _bdot(hh, wg_ref[c])
        u_ref[1] = _bdot(hh, wv_ref[c])

    def conv(u_ref, cw, cb):
        lo = FFN_HALO - 1
        return (cb + u_ref[lo:lo + tm] * cw[0:1] + u_ref[lo + 1:lo + 1 + tm] * cw[1:2]
                + u_ref[lo + 2:lo + 2 + tm] * cw[2:3])

    def consume(u_ref, c):
        gate = conv(u_ref.at[0], cwg_ref[c], cbg_ref[c])
        val = conv(u_ref.at[1], cwv_ref[c], cbv_ref[c])
        act = jax.nn.gelu(gate, approximate=True) * val
        acc_scr[...] += _bdot(act.astype(BF16), wd_ref[c])

    def body(jj, carry):
        produce(u_odd, 2 * jj + 1)
        consume(u_even, 2 * jj)
        produce(u_even, 2 * jj + 2)
        consume(u_odd, 2 * jj + 1)
        return carry

    n_pairs = (n_chunks - 1) // 2
    tail = n_chunks - 2 * n_pairs
    produce(u_even, 0)
    lax.fori_loop(0, n_pairs, body, 0)
    if tail == 2:
        produce(u_odd, n_chunks - 1)
    consume(u_even, 2 * n_pairs)
    if tail == 2:
        consume(u_odd, n_chunks - 1)
    o_ref[0] = x_ref[0] + _rms(acc_scr[...], pg_ref[...])


def _ffn(x, lw, tm):
    b, s, _ = x.shape
    n_chunks, _, fc = lw["w_gate"].shape
    r = tm // FFN_HALO
    n_halo = s // FFN_HALO
    tile = pl.BlockSpec((1, tm, D_MODEL), lambda bi, i: (bi, i, 0))
    prev = pl.BlockSpec((1, FFN_HALO, D_MODEL), lambda bi, i: (bi, jnp.maximum(i * r - 1, 0), 0))
    nxt = pl.BlockSpec((1, FFN_HALO, D_MODEL),
                       lambda bi, i: (bi, jnp.minimum((i + 1) * r, n_halo - 1), 0))
    return pl.pallas_call(
        functools.partial(_ffn_kernel, n_chunks=n_chunks),
        grid=(b, s // tm),
        in_specs=[tile, prev, nxt, _const_spec((1, D_MODEL)),
                  _const_spec((n_chunks, D_MODEL, fc)), _const_spec((n_chunks, D_MODEL, fc)),
                  _const_spec((n_chunks, 3, fc)), _const_spec((n_chunks, 3, fc)),
                  _const_spec((n_chunks, 1, fc)), _const_spec((n_chunks, 1, fc)),
                  _const_spec((n_chunks, fc, D_MODEL)), _const_spec((1, D_MODEL))],
        out_specs=tile,
        out_shape=jax.ShapeDtypeStruct(x.shape, F32),
        scratch_shapes=[pltpu.VMEM((tm + 2 * FFN_HALO, D_MODEL), BF16),
                        pltpu.VMEM((tm, D_MODEL), F32),
                        pltpu.VMEM((2, tm + 2 * FFN_HALO, fc), F32),
                        pltpu.VMEM((2, tm + 2 * FFN_HALO, fc), F32)],
        compiler_params=pltpu.CompilerParams(
            dimension_semantics=("parallel", "parallel"), vmem_limit_bytes=VMEM_LIMIT),
        name="conv_mlp",
    )(x, x, x, lw["ffn_pre_gain"], lw["w_gate"], lw["w_val"], lw["cw_gate"], lw["cw_val"],
      lw["cb_gate"], lw["cb_val"], lw["w_down"], lw["ffn_post_gain"])


def _rope_tables(s):
    t = jnp.arange(s)

    def angles(pos, dim):
        inv = ROPE_THETA ** (-jnp.arange(0, dim, 2, dtype=F32) / dim)
        return pos.astype(F32)[:, None] * inv[None, :]

    def lane_table(ang, lead, trail, reps):
        half = ang.shape[1]
        cos, sin, zero = jnp.cos(ang), jnp.sin(ang), jnp.zeros_like(ang)
        one = lambda n: jnp.ones((s, n), F32)
        nul = lambda n: jnp.zeros((s, n), F32)
        c = jnp.concatenate([one(lead)] + [cos, cos] * reps + [one(trail)], axis=1)
        lo = jnp.concatenate([nul(lead)] + [-sin, zero] * reps + [nul(trail)], axis=1)
        hi = jnp.concatenate([nul(lead)] + [zero, sin] * reps + [nul(trail)], axis=1)
        assert c.shape[1] == LANES and 2 * half * reps + lead + trail == LANES
        return jnp.stack([c, lo, hi])

    ang_row = angles(t // GRID_W, HEAD_DIM // 2)
    ang_col = angles(t % GRID_W, HEAD_DIM // 2)
    return {
        "mla": lane_table(angles(t, MLA_ROPE), MLA_NOPE, LANES - MLA_NOPE - MLA_ROPE, 1),
        "full": lane_table(angles(t, HEAD_DIM), 0, 0, LANES // HEAD_DIM),
        "axial": jnp.concatenate([jnp.cos(ang_row), jnp.sin(ang_row),
                                  jnp.cos(ang_col), jnp.sin(ang_col)], axis=1).T,
    }


def _layer_weights(l, mix_pre_gain, w_in, na_rpb, mla_q_gain, mla_w_uq, mla_kv_gain, mla_w_ukv,
                   ax_q_gain, ax_k_gain, sw_sink, w_out, mix_post_gain, ffn_pre_gain, w_up,
                   conv_w, conv_b, w_down, ffn_post_gain):
    gw, kvw = GROUP_WIDTH, GROUP_KV_HEADS * HEAD_DIM
    sizes = (gw, gw, gw, MLA_Q_RANK, MLA_KV_RANK, MLA_ROPE, gw, kvw, kvw, gw, kvw, kvw)
    bounds = np.cumsum((0,) + sizes)
    (a_q, a_k, a_v, b_cq, b_ckv, b_kr, c_q, c_k, c_v, d_q, d_k, d_v) = [
        w_in[l][:, bounds[j]:bounds[j + 1]] for j in range(len(sizes))]
    scale = HEAD_DIM ** -0.5
    zeros = lambda n: jnp.zeros((D_MODEL, n), F32)
    kr_block = jnp.concatenate([zeros(MLA_NOPE), b_kr, zeros(LANES - MLA_NOPE - MLA_ROPE)], axis=1)
    w_in_r = jnp.concatenate([a_q * scale, a_k, a_v, b_cq, b_ckv, kr_block, c_q, c_k, c_v,
                              d_q * scale, d_k, d_v], axis=1)
    assert w_in_r.shape[1] == _IN_COLS_PADDED

    uq = mla_w_uq[l].reshape(MLA_Q_RANK, GROUP_HEADS, MLA_NOPE + MLA_ROPE)
    uq = jnp.pad(uq, ((0, 0), (0, 0), (0, LANES - MLA_NOPE - MLA_ROPE)))
    ukv = mla_w_ukv[l].reshape(MLA_KV_RANK, GROUP_HEADS, MLA_NOPE + HEAD_DIM)
    uk = jnp.pad(ukv[:, :, :MLA_NOPE], ((0, 0), (0, 0), (0, LANES - MLA_NOPE)))
    uv = ukv[:, :, MLA_NOPE:]

    n_chunks = D_FF // FF_CHUNK
    chunk_cols = lambda w: w.reshape(w.shape[0], n_chunks, FF_CHUNK).transpose(1, 0, 2)
    row = lambda v: v[None, :].astype(F32)
    return {
        "pre_gain": row(mix_pre_gain[l]),
        "w_in": w_in_r.astype(BF16),
        "q_gain": row(mla_q_gain[l]),
        "w_uq": uq.reshape(MLA_Q_RANK, GROUP_HEADS * LANES).astype(BF16),
        "kv_gain": row(mla_kv_gain[l]),
        "w_uk": uk.reshape(MLA_KV_RANK, GROUP_HEADS * LANES).astype(BF16),
        "w_uv": uv.reshape(MLA_KV_RANK, GROUP_WIDTH).astype(BF16),
        "cq_gain": (ax_q_gain[l] * (scale * LOG2E))[:, None].astype(F32),
        "ck_gain": ax_k_gain[l][:, None].astype(F32),
        "na_bias": _na_bias_table(na_rpb[l]),
        "sink": sw_sink[l].astype(F32),
        "w_out": w_out[l].astype(BF16),
        "post_gain": row(mix_post_gain[l]),
        "ffn_pre_gain": row(ffn_pre_gain[l]),
        "w_gate": chunk_cols(w_up[l][:, :D_FF]).astype(BF16),
        "w_val": chunk_cols(w_up[l][:, D_FF:]).astype(BF16),
        "cw_gate": chunk_cols(conv_w[l][:, :D_FF]),
        "cw_val": chunk_cols(conv_w[l][:, D_FF:]),
        "cb_gate": chunk_cols(conv_b[l][None, :D_FF]),
        "cb_val": chunk_cols(conv_b[l][None, D_FF:]),
        "w_down": w_down[l].reshape(n_chunks, FF_CHUNK, D_MODEL).astype(BF16),
        "ffn_post_gain": row(ffn_post_gain[l]),
    }


def kernel(x, mix_pre_gain, w_in, na_rpb, mla_q_gain, mla_w_uq, mla_kv_gain, mla_w_ukv, ax_q_gain,
           ax_k_gain, sw_sink, w_out, mix_post_gain, ffn_pre_gain, w_up, conv_w, conv_b, w_down,
           ffn_post_gain):
    b, s, d = x.shape
    assert d == D_MODEL and s % max(TOKEN_TILE, FLASH_TQ, NA_ROWS_PER_STEP * GRID_W, SW_TQ) == 0
    assert s // GRID_W >= NA_WIN_ROWS
    params = (mix_pre_gain, w_in, na_rpb, mla_q_gain, mla_w_uq, mla_kv_gain, mla_w_ukv, ax_q_gain,
              ax_k_gain, sw_sink, w_out, mix_post_gain, ffn_pre_gain, w_up, conv_w, conv_b, w_down,
              ffn_post_gain)
    tabs = _rope_tables(s)
    for l in range(w_in.shape[0]):
        lw = _layer_weights(l, *params)
        (a_q, a_k, a_v, b_qT, b_k, b_vT, c_qT, c_k, c_vT, d_q, d_k, d_v) = _projection(
            x, lw, tabs, TOKEN_TILE)
        o_a = _neighbourhood(a_q, a_k, a_v, lw["na_bias"])
        o_bT = _flash(b_qT, b_k, b_vT, n_heads=GROUP_HEADS, n_kv=GROUP_HEADS, dk=LANES,
                      k_head_major=False, tq=FLASH_TQ * FLASH_STREAMS, n_streams=FLASH_STREAMS)
        o_cT = _flash(c_qT, c_k, c_vT, n_heads=GROUP_HEADS, n_kv=GROUP_KV_HEADS, dk=HEAD_DIM,
                      k_head_major=True, tq=FLASH_TQ * FLASH_STREAMS, n_streams=FLASH_STREAMS)
        o_d = _sliding_window(d_q, d_k, d_v, lw["sink"], SW_TQ)
        x = _out_projection(x, o_a, o_bT, o_cT, o_d, lw["w_out"], lw["post_gain"], TOKEN_TILE)
        x = _ffn(x, lw, TOKEN_TILE)
    return x
```

```python
import functools
import math

import numpy as np
import jax
import jax.numpy as jnp
from jax import lax
from jax.experimental import pallas as pl
from jax.experimental.pallas import tpu as pltpu

F32 = jnp.float32
BF16 = jnp.bfloat16

D_MODEL = 1024
GRID_W = 64
HEAD_DIM = 64
GROUP_HEADS = 4
GROUP_KV_HEADS = 2
GROUP_WIDTH = GROUP_HEADS * HEAD_DIM
ROPE_THETA = 10000.0
NORM_EPS = 1e-6
MASK_VALUE = -1e30
NA_WIN_ROWS = 8
NA_WIN_COLS = 16
MLA_Q_RANK = 256
MLA_KV_RANK = 128
MLA_NOPE = 64
MLA_ROPE = 32
SW_WINDOW = 128
D_FF = 2816
LOG2E = math.log2(math.e)

LANES = 128
VMEM_LIMIT = 56 * 1024 * 1024

TOKEN_TILE = 512
FLASH_TQ = 512
FLASH_STREAMS = 2
V_EXT = HEAD_DIM + 16
NA_ROWS_PER_STEP = 8
NA_ROWS_INTERLEAVED = 4
SW_TQ = 256
FF_CHUNK = 256


def _rms(x, gain):
    return x * lax.rsqrt(jnp.mean(x * x, axis=-1, keepdims=True) + NORM_EPS) * gain


def _rope_lanes(x, tab_ref, half):
    w = x.shape[-1]
    return (x * tab_ref[0] + pltpu.roll(x, w - half, 1) * tab_ref[1]
            + pltpu.roll(x, half, 1) * tab_ref[2])


def _bdot(a, b):
    return jnp.dot(a, b, preferred_element_type=F32)


_C_AQ, _C_AK, _C_AV = 0, 256, 512
_C_BCQ, _C_BCKV, _C_BKR = 768, 1024, 1152
_C_CQ, _C_CK, _C_CV = 1280, 1536, 1664
_C_DQ, _C_DK, _C_DV = 1792, 2048, 2176
_IN_COLS_PADDED = 2304
_PAIR = 2 * LANES


def _proj_kernel(x_ref, g_ref, win_ref, qg_ref, wuq_ref, kvg_ref, wuk_ref, wuv_ref,
                 cqg_ref, ckg_ref, tabb_ref, tabd_ref, tabc_ref,
                 aq_ref, ak_ref, av_ref, bqT_ref, bk_ref, bvT_ref,
                 cqT_ref, ck_ref, cvT_ref, dq_ref, dk_ref, dv_ref):
    h = _rms(x_ref[0], g_ref[...]).astype(BF16)

    def proj(c0, width):
        return _bdot(h, win_ref[:, c0:c0 + width])

    def store_heads(ref, z, n_heads):
        for hd in range(n_heads):
            ref[0, hd] = z[:, hd * HEAD_DIM:(hd + 1) * HEAD_DIM].astype(BF16)

    def store_vT_ext(ref, vT, n_heads):
        tm = vT.shape[1]
        pad = V_EXT - HEAD_DIM
        ones_row = (lax.broadcasted_iota(jnp.int32, (pad, tm), 0) == 0).astype(F32).astype(BF16)
        for hd in range(n_heads):
            ref[0, 0, hd * V_EXT:hd * V_EXT + HEAD_DIM, :] = (
                vT[hd * HEAD_DIM:(hd + 1) * HEAD_DIM].astype(BF16))
            ref[0, 0, hd * V_EXT + HEAD_DIM:(hd + 1) * V_EXT, :] = ones_row

    store_heads(aq_ref, proj(_C_AQ, GROUP_WIDTH), GROUP_HEADS)
    store_heads(ak_ref, proj(_C_AK, GROUP_WIDTH), GROUP_HEADS)
    store_heads(av_ref, proj(_C_AV, GROUP_WIDTH), GROUP_HEADS)

    cq = _rms(proj(_C_BCQ, MLA_Q_RANK), qg_ref[...]).astype(BF16)
    qb = _bdot(cq, wuq_ref[...]) * ((MLA_NOPE + MLA_ROPE) ** -0.5 * LOG2E)
    assert (_C_BKR, _C_CV, _C_DV) == (_C_BCKV + LANES, _C_CK + LANES, _C_DK + LANES)
    b_pair = proj(_C_BCKV, _PAIR)
    kpe = _rope_lanes(b_pair[:, LANES:], tabb_ref, MLA_ROPE // 2)
    ckv = _rms(b_pair[:, :LANES], kvg_ref[...]).astype(BF16)
    kn = _bdot(ckv, wuk_ref[...])
    for hd in range(GROUP_HEADS):
        blk = slice(hd * LANES, (hd + 1) * LANES)
        qh = _rope_lanes(qb[:, blk], tabb_ref, MLA_ROPE // 2)
        bqT_ref[0, blk, :] = qh.T.astype(BF16)
        bk_ref[0, :, blk] = (kn[:, blk] + kpe).astype(BF16)
    store_vT_ext(bvT_ref, _bdot(ckv, wuv_ref[...]).T, GROUP_HEADS)

    def norm_rope_T(blk, gain_col):
        ms = jnp.mean(blk * blk, axis=0, keepdims=True)
        blk = blk * lax.rsqrt(ms + NORM_EPS) * gain_col
        q = HEAD_DIM // 4
        cr, sr = tabc_ref[0:q], tabc_ref[q:2 * q]
        cc, sc = tabc_ref[2 * q:3 * q], tabc_ref[3 * q:4 * q]
        x1, x2, x3, x4 = blk[0:q], blk[q:2 * q], blk[2 * q:3 * q], blk[3 * q:4 * q]
        return jnp.concatenate([x1 * cr - x2 * sr, x2 * cr + x1 * sr,
                                x3 * cc - x4 * sc, x4 * cc + x3 * sc], axis=0)

    cqT = proj(_C_CQ, GROUP_WIDTH).T
    for hd in range(GROUP_HEADS):
        rows = slice(hd * HEAD_DIM, (hd + 1) * HEAD_DIM)
        cqT_ref[0, rows, :] = norm_rope_T(cqT[rows], cqg_ref[...]).astype(BF16)
    c_pair = proj(_C_CK, _PAIR)
    ckT = c_pair[:, :LANES].T
    ck = jnp.concatenate(
        [norm_rope_T(ckT[hd * HEAD_DIM:(hd + 1) * HEAD_DIM], ckg_ref[...])
         for hd in range(GROUP_KV_HEADS)], axis=0).T
    store_heads(ck_ref, ck, GROUP_KV_HEADS)
    store_vT_ext(cvT_ref, c_pair[:, LANES:].T, GROUP_KV_HEADS)

    dq = proj(_C_DQ, GROUP_WIDTH)
    dq = jnp.concatenate([_rope_lanes(dq[:, j * LANES:(j + 1) * LANES], tabd_ref, HEAD_DIM // 2)
                          for j in range(GROUP_WIDTH // LANES)], axis=1)
    store_heads(dq_ref, dq, GROUP_HEADS)
    d_pair = proj(_C_DK, _PAIR)
    dk = _rope_lanes(d_pair[:, :LANES], tabd_ref, HEAD_DIM // 2)
    store_heads(dk_ref, dk, GROUP_KV_HEADS)
    store_heads(dv_ref, d_pair[:, LANES:], GROUP_KV_HEADS)


def _const_spec(shape):
    n = len(shape)
    return pl.BlockSpec(shape, lambda *_: (0,) * n)


def _projection(x, lw, tabs, tm):
    b, s, _ = x.shape
    nt = s // tm
    kvw = GROUP_KV_HEADS * HEAD_DIM
    head_q = jax.ShapeDtypeStruct((b, GROUP_HEADS, s, HEAD_DIM), BF16)
    head_kv = jax.ShapeDtypeStruct((b, GROUP_KV_HEADS, s, HEAD_DIM), BF16)
    out_shape = (
        head_q, head_q, head_q,
        jax.ShapeDtypeStruct((b, GROUP_HEADS * LANES, s), BF16),
        jax.ShapeDtypeStruct((b, s, GROUP_HEADS * LANES), BF16),
        jax.ShapeDtypeStruct((b, nt, GROUP_HEADS * V_EXT, tm), BF16),
        jax.ShapeDtypeStruct((b, GROUP_WIDTH, s), BF16),
        head_kv,
        jax.ShapeDtypeStruct((b, nt, GROUP_KV_HEADS * V_EXT, tm), BF16),
        head_q, head_kv, head_kv,
    )
    hq_spec = pl.BlockSpec((1, GROUP_HEADS, tm, HEAD_DIM), lambda bi, i: (bi, 0, i, 0))
    hkv_spec = pl.BlockSpec((1, GROUP_KV_HEADS, tm, HEAD_DIM), lambda bi, i: (bi, 0, i, 0))
    out_specs = (
        hq_spec, hq_spec, hq_spec,
        pl.BlockSpec((1, GROUP_HEADS * LANES, tm), lambda bi, i: (bi, 0, i)),
        pl.BlockSpec((1, tm, GROUP_HEADS * LANES), lambda bi, i: (bi, i, 0)),
        pl.BlockSpec((1, 1, GROUP_HEADS * V_EXT, tm), lambda bi, i: (bi, i, 0, 0)),
        pl.BlockSpec((1, GROUP_WIDTH, tm), lambda bi, i: (bi, 0, i)),
        hkv_spec,
        pl.BlockSpec((1, 1, GROUP_KV_HEADS * V_EXT, tm), lambda bi, i: (bi, i, 0, 0)),
        hq_spec, hkv_spec, hkv_spec,
    )
    in_specs = [
        pl.BlockSpec((1, tm, D_MODEL), lambda bi, i: (bi, i, 0)),
        _const_spec((1, D_MODEL)),
        _const_spec((D_MODEL, _IN_COLS_PADDED)),
        _const_spec((1, MLA_Q_RANK)),
        _const_spec((MLA_Q_RANK, GROUP_HEADS * LANES)),
        _const_spec((1, MLA_KV_RANK)),
        _const_spec((MLA_KV_RANK, GROUP_HEADS * LANES)),
        _const_spec((MLA_KV_RANK, GROUP_WIDTH)),
        _const_spec((HEAD_DIM, 1)),
        _const_spec((HEAD_DIM, 1)),
        pl.BlockSpec((3, tm, LANES), lambda bi, i: (0, i, 0)),
        pl.BlockSpec((3, tm, LANES), lambda bi, i: (0, i, 0)),
        pl.BlockSpec((HEAD_DIM, tm), lambda bi, i: (0, i)),
    ]
    return pl.pallas_call(
        _proj_kernel,
        grid=(b, nt),
        in_specs=in_specs,
        out_specs=out_specs,
        out_shape=out_shape,
        compiler_params=pltpu.CompilerParams(
            dimension_semantics=("parallel", "parallel"), vmem_limit_bytes=VMEM_LIMIT),
        name="projection",
    )(x, lw["pre_gain"], lw["w_in"], lw["q_gain"], lw["w_uq"], lw["kv_gain"], lw["w_uk"],
      lw["w_uv"], lw["cq_gain"], lw["ck_gain"], tabs["mla"], tabs["full"], tabs["axial"])


def _flash_kernel(qT_ref, k_ref, vT_ref, oT_ref, s_even, s_odd, *, tk, n_chunks, n_streams):
    tq = qT_ref.shape[2] // n_streams
    dv_ext = vT_ref.shape[2]

    def produce(s_ref, c):
        k = k_ref[0, pl.ds(pl.multiple_of(c * tk, tk), tk), :]
        maxes = []
        for st in range(n_streams):
            s = _bdot(k, qT_ref[0, :, st * tq:(st + 1) * tq])
            s_ref[st] = s
            maxes.append(jnp.max(s, axis=0, keepdims=True))
        return tuple(maxes)

    def consume(s_ref, chunk_max, c, carry):
        vT = vT_ref[0, c]
        out = []
        for st, (m, acc) in enumerate(carry):
            m_new = jnp.maximum(m, chunk_max[st])
            p = jnp.exp2(s_ref[st] - m_new).astype(BF16)
            acc = jnp.exp2(m - m_new) * acc + _bdot(vT, p)
            out.append((m_new, acc))
        return tuple(out)

    def body(jj, state):
        carry, max_even = state
        max_odd = produce(s_odd, 2 * jj + 1)
        carry = consume(s_even, max_even, 2 * jj, carry)
        max_even = produce(s_even, 2 * jj + 2)
        return consume(s_odd, max_odd, 2 * jj + 1, carry), max_even

    carry = tuple((jnp.full((1, tq), -jnp.inf, F32), jnp.zeros((dv_ext, tq), F32))
                  for _ in range(n_streams))
    max_even = produce(s_even, 0)
    carry, max_even = lax.fori_loop(0, n_chunks // 2 - 1, body, (carry, max_even))
    max_odd = produce(s_odd, n_chunks - 1)
    carry = consume(s_even, max_even, n_chunks - 2, carry)
    carry = consume(s_odd, max_odd, n_chunks - 1, carry)
    for st, (_, acc) in enumerate(carry):
        oT_ref[0, :, st * tq:(st + 1) * tq] = acc[:HEAD_DIM] / acc[HEAD_DIM:HEAD_DIM + 1]


def _flash(qT, k, vT, *, n_heads, n_kv, dk, k_head_major, tq, n_streams):
    b, _, s = qT.shape
    n_chunks, tk = vT.shape[1], vT.shape[3]
    assert n_chunks % 2 == 0 and s % tq == 0
    dv = HEAD_DIM
    rep = n_heads // n_kv
    if k_head_major:
        k_spec = pl.BlockSpec((None, 1, s, dk), lambda bi, h, i: (bi, h // rep, 0, 0))
    else:
        k_spec = pl.BlockSpec((1, s, dk), lambda bi, h, i: (bi, 0, h // rep))
    return pl.pallas_call(
        functools.partial(_flash_kernel, tk=tk, n_chunks=n_chunks, n_streams=n_streams),
        grid=(b, n_heads, s // tq),
        in_specs=[
            pl.BlockSpec((1, dk, tq), lambda bi, h, i: (bi, h, i)),
            k_spec,
            pl.BlockSpec((1, n_chunks, V_EXT, tk), lambda bi, h, i: (bi, 0, h // rep, 0)),
        ],
        out_specs=pl.BlockSpec((1, dv, tq), lambda bi, h, i: (bi, h, i)),
        out_shape=jax.ShapeDtypeStruct((b, n_heads * dv, s), F32),
        scratch_shapes=[pltpu.VMEM((n_streams, tk, tq // n_streams), F32)] * 2,
        compiler_params=pltpu.CompilerParams(
            dimension_semantics=("parallel", "parallel", "parallel"),
            vmem_limit_bytes=VMEM_LIMIT),
        name="dense_attention",
    )(qT, k, vT)


def _na_kernel(q_ref, kp_ref, kc_ref, kn_ref, vp_ref, vc_ref, vn_ref, bias_ref, o_ref,
               k_win, v_win, *, n_rows):
    i = pl.program_id(1)
    blk = NA_ROWS_PER_STEP * GRID_W
    band = NA_WIN_ROWS * GRID_W
    for w, (kr, vr) in enumerate(((kp_ref, vp_ref), (kc_ref, vc_ref), (kn_ref, vn_ref))):
        k_win[:, w * blk:(w + 1) * blk, :] = kr[0]
        v_win[:, w * blk:(w + 1) * blk, :] = vr[0]
    heads = range(GROUP_HEADS)
    for j0 in range(0, NA_ROWS_PER_STEP, NA_ROWS_INTERLEAVED):
        rows = range(j0, j0 + NA_ROWS_INTERLEAVED)
        offs, ds = {}, {}
        for j in rows:
            r = i * NA_ROWS_PER_STEP + j
            rs = jnp.clip(r - NA_WIN_ROWS // 2, 0, n_rows - NA_WIN_ROWS)
            offs[j] = pl.multiple_of((rs - (i - 1) * NA_ROWS_PER_STEP) * GRID_W, GRID_W)
            ds[j] = r - rs
        ss = {(j, hd): lax.dot_general(q_ref[0, hd, j * GRID_W:(j + 1) * GRID_W, :],
                                       k_win[hd, pl.ds(offs[j], band), :],
                                       (((1,), (1,)), ((), ())), preferred_element_type=F32)
              + bias_ref[hd, ds[j]] for j in rows for hd in heads}
        ps = {key: jnp.exp(s - jnp.max(s, axis=-1, keepdims=True)) for key, s in ss.items()}
        for j in rows:
            os = [_bdot(ps[j, hd].astype(BF16), v_win[hd, pl.ds(offs[j], band), :])
                  / jnp.sum(ps[j, hd], axis=-1, keepdims=True) for hd in heads]
            o_ref[0, j * GRID_W:(j + 1) * GRID_W, :] = jnp.concatenate(os, axis=1)


def _neighbourhood(q, k, v, bias):
    b, nh, s, hd = q.shape
    blk = NA_ROWS_PER_STEP * GRID_W
    nb = s // blk
    n_rows = s // GRID_W
    cur = lambda bi, i: (bi, 0, i, 0)
    prev = lambda bi, i: (bi, 0, jnp.maximum(i - 1, 0), 0)
    nxt = lambda bi, i: (bi, 0, jnp.minimum(i + 1, nb - 1), 0)
    spec = lambda f: pl.BlockSpec((1, nh, blk, hd), f)
    return pl.pallas_call(
        functools.partial(_na_kernel, n_rows=n_rows),
        grid=(b, nb),
        in_specs=[spec(cur), spec(prev), spec(cur), spec(nxt), spec(prev), spec(cur), spec(nxt),
                  _const_spec(bias.shape)],
        out_specs=pl.BlockSpec((1, blk, nh * hd), lambda bi, i: (bi, i, 0)),
        out_shape=jax.ShapeDtypeStruct((b, s, nh * hd), F32),
        scratch_shapes=[pltpu.VMEM((nh, 3 * blk, hd), BF16), pltpu.VMEM((nh, 3 * blk, hd), BF16)],
        compiler_params=pltpu.CompilerParams(
            dimension_semantics=("parallel", "parallel"), vmem_limit_bytes=VMEM_LIMIT),
        name="neighbourhood_attention",
    )(q, k, k, k, v, v, v, bias)


def _na_bias_table(rpb):
    c = np.arange(GRID_W)[:, None]
    kc = np.arange(GRID_W)[None, :]
    cs = np.clip(c - NA_WIN_COLS // 2, 0, GRID_W - NA_WIN_COLS)
    valid = (kc >= cs) & (kc < cs + NA_WIN_COLS)
    col_off = kc - c + (NA_WIN_COLS - 1)
    n_off = 2 * NA_WIN_COLS - 1
    select = (valid[:, :, None] & (col_off[:, :, None] == np.arange(n_off))).astype(np.float32)
    x = jnp.einsum("hro,cko->hrck", rpb.astype(F32), jnp.asarray(select),
                   precision=lax.Precision.HIGHEST)
    x = jnp.where(valid[None, None], x, MASK_VALUE)
    t = jnp.stack([x[:, NA_WIN_ROWS - 1 - d:2 * NA_WIN_ROWS - 1 - d] for d in range(NA_WIN_ROWS)],
                  axis=1)
    t = t.transpose(0, 1, 3, 2, 4)
    return t.reshape(rpb.shape[0], NA_WIN_ROWS, GRID_W, NA_WIN_ROWS * GRID_W)


def _sw_kernel(sink_ref, q_ref, kp_ref, kc_ref, kn_ref, vp_ref, vc_ref, vn_ref, o_ref, *, seq):
    i = pl.program_id(1)
    tq = q_ref.shape[2]
    span = tq + 2 * SW_WINDOW
    t0 = i * tq
    row = lax.broadcasted_iota(jnp.int32, (tq, span), 0)
    col = lax.broadcasted_iota(jnp.int32, (tq, span), 1)
    kpos = col + (t0 - SW_WINDOW)
    rel = col - row
    valid = (rel >= 0) & (rel <= 2 * SW_WINDOW) & (kpos >= 0) & (kpos < seq)
    nt = (((1,), (1,)), ((), ()))
    rep = GROUP_HEADS // GROUP_KV_HEADS
    heads = range(GROUP_HEADS)
    ss = [jnp.where(valid, jnp.concatenate(
        [lax.dot_general(q_ref[0, hd], kr[0, hd // rep], nt, preferred_element_type=F32)
         for kr in (kp_ref, kc_ref, kn_ref)], axis=1), MASK_VALUE) for hd in heads]
    ms = [jnp.maximum(jnp.max(ss[hd], axis=-1, keepdims=True), sink_ref[hd]) for hd in heads]
    ps = [jnp.exp(ss[hd] - ms[hd]) for hd in heads]
    denoms = [jnp.sum(ps[hd], axis=-1, keepdims=True) + jnp.exp(sink_ref[hd] - ms[hd])
              for hd in heads]
    os = []
    for hd in heads:
        g = hd // rep
        pb = ps[hd].astype(BF16)
        o = (_bdot(pb[:, :SW_WINDOW], vp_ref[0, g])
             + _bdot(pb[:, SW_WINDOW:SW_WINDOW + tq], vc_ref[0, g])
             + _bdot(pb[:, SW_WINDOW + tq:], vn_ref[0, g]))
        os.append(o / denoms[hd])
    o_ref[0] = jnp.concatenate(os, axis=1)


def _sliding_window(q, k, v, sink, tq):
    b, nh, s, hd = q.shape
    nkv = k.shape[1]
    nb = s // tq
    r = tq // SW_WINDOW
    n_small = s // SW_WINDOW
    cur = pl.BlockSpec((1, nkv, tq, hd), lambda bi, i: (bi, 0, i, 0))
    prev = pl.BlockSpec((1, nkv, SW_WINDOW, hd), lambda bi, i: (bi, 0, jnp.maximum(i * r - 1, 0), 0))
    nxt = pl.BlockSpec((1, nkv, SW_WINDOW, hd),
                       lambda bi, i: (bi, 0, jnp.minimum((i + 1) * r, n_small - 1), 0))
    return pl.pallas_call(
        functools.partial(_sw_kernel, seq=s),
        grid=(b, nb),
        in_specs=[pl.BlockSpec(memory_space=pltpu.SMEM),
                  pl.BlockSpec((1, nh, tq, hd), lambda bi, i: (bi, 0, i, 0)),
                  prev, cur, nxt, prev, cur, nxt],
        out_specs=pl.BlockSpec((1, tq, nh * hd), lambda bi, i: (bi, i, 0)),
        out_shape=jax.ShapeDtypeStruct((b, s, nh * hd), F32),
        compiler_params=pltpu.CompilerParams(
            dimension_semantics=("parallel", "parallel"), vmem_limit_bytes=VMEM_LIMIT),
        name="sliding_window_attention",
    )(sink, q, k, k, k, v, v, v)


def _out_kernel(x_ref, oa_ref, obT_ref, ocT_ref, od_ref, w_ref, g_ref, o_ref):
    mixed_in = jnp.concatenate(
        [oa_ref[0].astype(BF16), obT_ref[0].T.astype(BF16), ocT_ref[0].T.astype(BF16),
         od_ref[0].astype(BF16)], axis=1)
    mixed = _bdot(mixed_in, w_ref[...])
    o_ref[0] = x_ref[0] + _rms(mixed, g_ref[...])


def _out_projection(x, o_a, o_bT, o_cT, o_d, w_out, gain, tm):
    b, s, _ = x.shape
    tok = lambda w: pl.BlockSpec((1, tm, w), lambda bi, i: (bi, i, 0))
    feat = pl.BlockSpec((1, GROUP_WIDTH, tm), lambda bi, i: (bi, 0, i))
    return pl.pallas_call(
        _out_kernel,
        grid=(b, s // tm),
        in_specs=[tok(D_MODEL), tok(GROUP_WIDTH), feat, feat, tok(GROUP_WIDTH),
                  _const_spec((D_MODEL, D_MODEL)), _const_spec((1, D_MODEL))],
        out_specs=tok(D_MODEL),
        out_shape=jax.ShapeDtypeStruct(x.shape, F32),
        compiler_params=pltpu.CompilerParams(
            dimension_semantics=("parallel", "parallel"), vmem_limit_bytes=VMEM_LIMIT),
        name="out_projection",
    )(x, o_a, o_bT, o_cT, o_d, w_out, gain)


FFN_HALO = 8


def _ffn_kernel(x_ref, xp_ref, xn_ref, g_ref, wg_ref, wv_ref, cwg_ref, cwv_ref, cbg_ref, cbv_ref,
                wd_ref, pg_ref, o_ref, h_scr, acc_scr, u_even, u_odd, *, n_chunks):
    i = pl.program_id(1)
    n_tiles = pl.num_programs(1)
    tm = x_ref.shape[1]
    ext = tm + 2 * FFN_HALO
    g = g_ref[...]
    hp = _rms(xp_ref[0], g) * (i > 0).astype(F32)
    hn = _rms(xn_ref[0], g) * (i < n_tiles - 1).astype(F32)
    h_scr[...] = jnp.concatenate([hp, _rms(x_ref[0], g), hn], axis=0).astype(BF16)
    acc_scr[...] = jnp.zeros_like(acc_scr)

    def produce(u_ref, c):
        hh = h_scr[...]
        u_ref[0] = _bdot(hh, wg_ref[c])
        u_ref[1] = _bdot(hh, wv_ref[c])

    def conv(u_ref, cw, cb):
        lo = FFN_HALO - 1
        return (cb + u_ref[lo:lo + tm] * cw[0:1] + u_ref[lo + 1:lo + 1 + tm] * cw[1:2]
                + u_ref[lo + 2:lo + 2 + tm] * cw[2:3])

    def consume(u_ref, c):
        gate = conv(u_ref.at[0], cwg_ref[c], cbg_ref[c])
        val = conv(u_ref.at[1], cwv_ref[c], cbv_ref[c])
        act = jax.nn.gelu(gate, approximate=True) * val
        acc_scr[...] += _bdot(act.astype(BF16), wd_ref[c])

    def body(jj, carry):
        produce(u_odd, 2 * jj + 1)
        consume(u_even, 2 * jj)
        produce(u_even, 2 * jj + 2)
        consume(u_odd, 2 * jj + 1)
        return carry

    n_pairs = (n_chunks - 1) // 2
    tail = n_chunks - 2 * n_pairs
    produce(u_even, 0)
    lax.fori_loop(0, n_pairs, body, 0)
    if tail == 2:
        produce(u_odd, n_chunks - 1)
    consume(u_even, 2 * n_pairs)
    if tail == 2:
        consume(u_odd, n_chunks - 1)
    o_ref[0] = x_ref[0] + _rms(acc_scr[...], pg_ref[...])


def _ffn(x, lw, tm):
    b, s, _ = x.shape
    n_chunks, _, fc = lw["w_gate"].shape
    r = tm // FFN_HALO
    n_halo = s // FFN_HALO
    tile = pl.BlockSpec((1, tm, D_MODEL), lambda bi, i: (bi, i, 0))
    prev = pl.BlockSpec((1, FFN_HALO, D_MODEL), lambda bi, i: (bi, jnp.maximum(i * r - 1, 0), 0))
    nxt = pl.BlockSpec((1, FFN_HALO, D_MODEL),
                       lambda bi, i: (bi, jnp.minimum((i + 1) * r, n_halo - 1), 0))
    return pl.pallas_call(
        functools.partial(_ffn_kernel, n_chunks=n_chunks),
        grid=(b, s // tm),
        in_specs=[tile, prev, nxt, _const_spec((1, D_MODEL)),
                  _const_spec((n_chunks, D_MODEL, fc)), _const_spec((n_chunks, D_MODEL, fc)),
                  _const_spec((n_chunks, 3, fc)), _const_spec((n_chunks, 3, fc)),
                  _const_spec((n_chunks, 1, fc)), _const_spec((n_chunks, 1, fc)),
                  _const_spec((n_chunks, fc, D_MODEL)), _const_spec((1, D_MODEL))],
        out_specs=tile,
        out_shape=jax.ShapeDtypeStruct(x.shape, F32),
        scratch_shapes=[pltpu.VMEM((tm + 2 * FFN_HALO, D_MODEL), BF16),
                        pltpu.VMEM((tm, D_MODEL), F32),
                        pltpu.VMEM((2, tm + 2 * FFN_HALO, fc), F32),
                        pltpu.VMEM((2, tm + 2 * FFN_HALO, fc), F32)],
        compiler_params=pltpu.CompilerParams(
            dimension_semantics=("parallel", "parallel"), vmem_limit_bytes=VMEM_LIMIT),
        name="conv_mlp",
    )(x, x, x, lw["ffn_pre_gain"], lw["w_gate"], lw["w_val"], lw["cw_gate"], lw["cw_val"],
      lw["cb_gate"], lw["cb_val"], lw["w_down"], lw["ffn_post_gain"])


def _rope_tables(s):
    t = jnp.arange(s)

    def angles(pos, dim):
        inv = ROPE_THETA ** (-jnp.arange(0, dim, 2, dtype=F32) / dim)
        return pos.astype(F32)[:, None] * inv[None, :]

    def lane_table(ang, lead, trail, reps):
        half = ang.shape[1]
        cos, sin, zero = jnp.cos(ang), jnp.sin(ang), jnp.zeros_like(ang)
        one = lambda n: jnp.ones((s, n), F32)
        nul = lambda n: jnp.zeros((s, n), F32)
        c = jnp.concatenate([one(lead)] + [cos, cos] * reps + [one(trail)], axis=1)
        lo = jnp.concatenate([nul(lead)] + [-sin, zero] * reps + [nul(trail)], axis=1)
        hi = jnp.concatenate([nul(lead)] + [zero, sin] * reps + [nul(trail)], axis=1)
        assert c.shape[1] == LANES and 2 * half * reps + lead + trail == LANES
        return jnp.stack([c, lo, hi])

    ang_row = angles(t // GRID_W, HEAD_DIM // 2)
    ang_col = angles(t % GRID_W, HEAD_DIM // 2)
    return {
        "mla": lane_table(angles(t, MLA_ROPE), MLA_NOPE, LANES - MLA_NOPE - MLA_ROPE, 1),
        "full": lane_table(angles(t, HEAD_DIM), 0, 0, LANES // HEAD_DIM),
        "axial": jnp.concatenate([jnp.cos(ang_row), jnp.sin(ang_row),
                                  jnp.cos(ang_col), jnp.sin(ang_col)], axis=1).T,
    }


def _layer_weights(l, mix_pre_gain, w_in, na_rpb, mla_q_gain, mla_w_uq, mla_kv_gain, mla_w_ukv,
                   ax_q_gain, ax_k_gain, sw_sink, w_out, mix_post_gain, ffn_pre_gain, w_up,
                   conv_w, conv_b, w_down, ffn_post_gain):
    gw, kvw = GROUP_WIDTH, GROUP_KV_HEADS * HEAD_DIM
    sizes = (gw, gw, gw, MLA_Q_RANK, MLA_KV_RANK, MLA_ROPE, gw, kvw, kvw, gw, kvw, kvw)
    bounds = np.cumsum((0,) + sizes)
    (a_q, a_k, a_v, b_cq, b_ckv, b_kr, c_q, c_k, c_v, d_q, d_k, d_v) = [
        w_in[l][:, bounds[j]:bounds[j + 1]] for j in range(len(sizes))]
    scale = HEAD_DIM ** -0.5
    zeros = lambda n: jnp.zeros((D_MODEL, n), F32)
    kr_block = jnp.concatenate([zeros(MLA_NOPE), b_kr, zeros(LANES - MLA_NOPE - MLA_ROPE)], axis=1)
    w_in_r = jnp.concatenate([a_q * scale, a_k, a_v, b_cq, b_ckv, kr_block, c_q, c_k, c_v,
                              d_q * scale, d_k, d_v], axis=1)
    assert w_in_r.shape[1] == _IN_COLS_PADDED

    uq = mla_w_uq[l].reshape(MLA_Q_RANK, GROUP_HEADS, MLA_NOPE + MLA_ROPE)
    uq = jnp.pad(uq, ((0, 0), (0, 0), (0, LANES - MLA_NOPE - MLA_ROPE)))
    ukv = mla_w_ukv[l].reshape(MLA_KV_RANK, GROUP_HEADS, MLA_NOPE + HEAD_DIM)
    uk = jnp.pad(ukv[:, :, :MLA_NOPE], ((0, 0), (0, 0), (0, LANES - MLA_NOPE)))
    uv = ukv[:, :, MLA_NOPE:]

    n_chunks = D_FF // FF_CHUNK
    chunk_cols = lambda w: w.reshape(w.shape[0], n_chunks, FF_CHUNK).transpose(1, 0, 2)
    row = lambda v: v[None, :].astype(F32)
    return {
        "pre_gain": row(mix_pre_gain[l]),
        "w_in": w_in_r.astype(BF16),
        "q_gain": row(mla_q_gain[l]),
        "w_uq": uq.reshape(MLA_Q_RANK, GROUP_HEADS * LANES).astype(BF16),
        "kv_gain": row(mla_kv_gain[l]),
        "w_uk": uk.reshape(MLA_KV_RANK, GROUP_HEADS * LANES).astype(BF16),
        "w_uv": uv.reshape(MLA_KV_RANK, GROUP_WIDTH).astype(BF16),
        "cq_gain": (ax_q_gain[l] * (scale * LOG2E))[:, None].astype(F32),
        "ck_gain": ax_k_gain[l][:, None].astype(F32),
        "na_bias": _na_bias_table(na_rpb[l]),
        "sink": sw_sink[l].astype(F32),
        "w_out": w_out[l].astype(BF16),
        "post_gain": row(mix_post_gain[l]),
        "ffn_pre_gain": row(ffn_pre_gain[l]),
        "w_gate": chunk_cols(w_up[l][:, :D_FF]).astype(BF16),
        "w_val": chunk_cols(w_up[l][:, D_FF:]).astype(BF16),
        "cw_gate": chunk_cols(conv_w[l][:, :D_FF]),
        "cw_val": chunk_cols(conv_w[l][:, D_FF:]),
        "cb_gate": chunk_cols(conv_b[l][None, :D_FF]),
        "cb_val": chunk_cols(conv_b[l][None, D_FF:]),
        "w_down": w_down[l].reshape(n_chunks, FF_CHUNK, D_MODEL).astype(BF16),
        "ffn_post_gain": row(ffn_post_gain[l]),
    }


def kernel(x, mix_pre_gain, w_in, na_rpb, mla_q_gain, mla_w_uq, mla_kv_gain, mla_w_ukv, ax_q_gain,
           ax_k_gain, sw_sink, w_out, mix_post_gain, ffn_pre_gain, w_up, conv_w, conv_b, w_down,
           ffn_post_gain):
    b, s, d = x.shape
    assert d == D_MODEL and s % max(TOKEN_TILE, FLASH_TQ, NA_ROWS_PER_STEP * GRID_W, SW_TQ) == 0
    assert s // GRID_W >= NA_WIN_ROWS
    params = (mix_pre_gain, w_in, na_rpb, mla_q_gain, mla_w_uq, mla_kv_gain, mla_w_ukv, ax_q_gain,
              ax_k_gain, sw_sink, w_out, mix_post_gain, ffn_pre_gain, w_up, conv_w, conv_b, w_down,
              ffn_post_gain)
    tabs = _rope_tables(s)
    for l in range(w_in.shape[0]):
        lw = _layer_weights(l, *params)
        (a_q, a_k, a_v, b_qT, b_k, b_vT, c_qT, c_k, c_vT, d_q, d_k, d_v) = _projection(
            x, lw, tabs, TOKEN_TILE)
        o_a = _neighbourhood(a_q, a_k, a_v, lw["na_bias"])
        o_bT = _flash(b_qT, b_k, b_vT, n_heads=GROUP_HEADS, n_kv=GROUP_HEADS, dk=LANES,
                      k_head_major=False, tq=FLASH_TQ * FLASH_STREAMS, n_streams=FLASH_STREAMS)
        o_cT = _flash(c_qT, c_k, c_vT, n_heads=GROUP_HEADS, n_kv=GROUP_KV_HEADS, dk=HEAD_DIM,
                      k_head_major=True, tq=FLASH_TQ * FLASH_STREAMS, n_streams=FLASH_STREAMS)
        o_d = _sliding_window(d_q, d_k, d_v, lw["sink"], SW_TQ)
        x = _out_projection(x, o_a, o_bT, o_cT, o_d, lw["w_out"], lw["post_gain"], TOKEN_TILE)
        x = _ffn(x, lw, TOKEN_TILE)
    return x
```

```python
import functools
import math

import numpy as np
import jax
import jax.numpy as jnp
from jax import lax
from jax.experimental import pallas as pl
from jax.experimental.pallas import tpu as pltpu

F32 = jnp.float32
BF16 = jnp.bfloat16

D_MODEL = 1024
GRID_W = 64
HEAD_DIM = 64
GROUP_HEADS = 4
GROUP_KV_HEADS = 2
GROUP_WIDTH = GROUP_HEADS * HEAD_DIM
ROPE_THETA = 10000.0
NORM_EPS = 1e-6
MASK_VALUE = -1e30
NA_WIN_ROWS = 8
NA_WIN_COLS = 16
MLA_Q_RANK = 256
MLA_KV_RANK = 128
MLA_NOPE = 64
MLA_ROPE = 32
SW_WINDOW = 128
D_FF = 2816
LOG2E = math.log2(math.e)

LANES = 128
VMEM_LIMIT = 56 * 1024 * 1024

TOKEN_TILE = 512
FLASH_TQ = 512
FLASH_STREAMS = 2
FLASH_PAIRS_PER_TRIP = 3
V_EXT = HEAD_DIM + 16
NA_ROWS_PER_STEP = 8
NA_ROWS_INTERLEAVED = 4
SW_TQ = 256
FFN_PAIRS_PER_TRIP = 1
FF_CHUNK = 256


def _rms(x, gain):
    return x * lax.rsqrt(jnp.mean(x * x, axis=-1, keepdims=True) + NORM_EPS) * gain


def _rope_lanes(x, tab_ref, half):
    w = x.shape[-1]
    return (x * tab_ref[0] + pltpu.roll(x, w - half, 1) * tab_ref[1]
            + pltpu.roll(x, half, 1) * tab_ref[2])


def _bdot(a, b):
    return jnp.dot(a, b, preferred_element_type=F32)


_C_AQ, _C_AK, _C_AV = 0, 256, 512
_C_BCQ, _C_BCKV, _C_BKR = 768, 1024, 1152
_C_CQ, _C_CK, _C_CV = 1280, 1536, 1664
_C_DQ, _C_DK, _C_DV = 1792, 2048, 2176
_IN_COLS_PADDED = 2304
_PAIR = 2 * LANES


def _proj_kernel(x_ref, g_ref, win_ref, qg_ref, wuq_ref, kvg_ref, wuk_ref, wuv_ref,
                 cqg_ref, ckg_ref, tabb_ref, tabd_ref, tabc_ref,
                 aq_ref, ak_ref, av_ref, bqT_ref, bk_ref, bvT_ref,
                 cqT_ref, ck_ref, cvT_ref, dq_ref, dk_ref, dv_ref):
    h = _rms(x_ref[0], g_ref[...]).astype(BF16)

    def proj(c0, width):
        return _bdot(h, win_ref[:, c0:c0 + width])

    def store_heads(ref, z, n_heads):
        for hd in range(n_heads):
            ref[0, hd] = z[:, hd * HEAD_DIM:(hd + 1) * HEAD_DIM].astype(BF16)

    def store_vT_ext(ref, vT, n_heads):
        tm = vT.shape[1]
        pad = V_EXT - HEAD_DIM
        ones_row = (lax.broadcasted_iota(jnp.int32, (pad, tm), 0) == 0).astype(F32).astype(BF16)
        for hd in range(n_heads):
            ref[0, 0, hd * V_EXT:hd * V_EXT + HEAD_DIM, :] = (
                vT[hd * HEAD_DIM:(hd + 1) * HEAD_DIM].astype(BF16))
            ref[0, 0, hd * V_EXT + HEAD_DIM:(hd + 1) * V_EXT, :] = ones_row

    store_heads(aq_ref, proj(_C_AQ, GROUP_WIDTH), GROUP_HEADS)
    store_heads(ak_ref, proj(_C_AK, GROUP_WIDTH), GROUP_HEADS)
    store_heads(av_ref, proj(_C_AV, GROUP_WIDTH), GROUP_HEADS)

    cq = _rms(proj(_C_BCQ, MLA_Q_RANK), qg_ref[...]).astype(BF16)
    qb = _bdot(cq, wuq_ref[...]) * ((MLA_NOPE + MLA_ROPE) ** -0.5 * LOG2E)
    assert (_C_BKR, _C_CV, _C_DV) == (_C_BCKV + LANES, _C_CK + LANES, _C_DK + LANES)
    b_pair = proj(_C_BCKV, _PAIR)
    kpe = _rope_lanes(b_pair[:, LANES:], tabb_ref, MLA_ROPE // 2)
    ckv = _rms(b_pair[:, :LANES], kvg_ref[...]).astype(BF16)
    kn = _bdot(ckv, wuk_ref[...])
    for hd in range(GROUP_HEADS):
        blk = slice(hd * LANES, (hd + 1) * LANES)
        qh = _rope_lanes(qb[:, blk], tabb_ref, MLA_ROPE // 2)
        bqT_ref[0, blk, :] = qh.T.astype(BF16)
        bk_ref[0, :, blk] = (kn[:, blk] + kpe).astype(BF16)
    store_vT_ext(bvT_ref, _bdot(ckv, wuv_ref[...]).T, GROUP_HEADS)

    def norm_rope_T(blk, gain_col):
        ms = jnp.mean(blk * blk, axis=0, keepdims=True)
        blk = blk * lax.rsqrt(ms + NORM_EPS) * gain_col
        q = HEAD_DIM // 4
        cr, sr = tabc_ref[0:q], tabc_ref[q:2 * q]
        cc, sc = tabc_ref[2 * q:3 * q], tabc_ref[3 * q:4 * q]
        x1, x2, x3, x4 = blk[0:q], blk[q:2 * q], blk[2 * q:3 * q], blk[3 * q:4 * q]
        return jnp.concatenate([x1 * cr - x2 * sr, x2 * cr + x1 * sr,
                                x3 * cc - x4 * sc, x4 * cc + x3 * sc], axis=0)

    cqT = proj(_C_CQ, GROUP_WIDTH).T
    for hd in range(GROUP_HEADS):
        rows = slice(hd * HEAD_DIM, (hd + 1) * HEAD_DIM)
        cqT_ref[0, rows, :] = norm_rope_T(cqT[rows], cqg_ref[...]).astype(BF16)
    c_pair = proj(_C_CK, _PAIR)
    ckT = c_pair[:, :LANES].T
    ck = jnp.concatenate(
        [norm_rope_T(ckT[hd * HEAD_DIM:(hd + 1) * HEAD_DIM], ckg_ref[...])
         for hd in range(GROUP_KV_HEADS)], axis=0).T
    store_heads(ck_ref, ck, GROUP_KV_HEADS)
    store_vT_ext(cvT_ref, c_pair[:, LANES:].T, GROUP_KV_HEADS)

    dq = proj(_C_DQ, GROUP_WIDTH)
    dq = jnp.concatenate([_rope_lanes(dq[:, j * LANES:(j + 1) * LANES], tabd_ref, HEAD_DIM // 2)
                          for j in range(GROUP_WIDTH // LANES)], axis=1)
    store_heads(dq_ref, dq, GROUP_HEADS)
    d_pair = proj(_C_DK, _PAIR)
    dk = _rope_lanes(d_pair[:, :LANES], tabd_ref, HEAD_DIM // 2)
    store_heads(dk_ref, dk, GROUP_KV_HEADS)
    store_heads(dv_ref, d_pair[:, LANES:], GROUP_KV_HEADS)


def _const_spec(shape):
    n = len(shape)
    return pl.BlockSpec(shape, lambda *_: (0,) * n)


def _projection(x, lw, tabs, tm):
    b, s, _ = x.shape
    nt = s // tm
    kvw = GROUP_KV_HEADS * HEAD_DIM
    head_q = jax.ShapeDtypeStruct((b, GROUP_HEADS, s, HEAD_DIM), BF16)
    head_kv = jax.ShapeDtypeStruct((b, GROUP_KV_HEADS, s, HEAD_DIM), BF16)
    out_shape = (
        head_q, head_q, head_q,
        jax.ShapeDtypeStruct((b, GROUP_HEADS * LANES, s), BF16),
        jax.ShapeDtypeStruct((b, s, GROUP_HEADS * LANES), BF16),
        jax.ShapeDtypeStruct((b, nt, GROUP_HEADS * V_EXT, tm), BF16),
        jax.ShapeDtypeStruct((b, GROUP_WIDTH, s), BF16),
        head_kv,
        jax.ShapeDtypeStruct((b, nt, GROUP_KV_HEADS * V_EXT, tm), BF16),
        head_q, head_kv, head_kv,
    )
    hq_spec = pl.BlockSpec((1, GROUP_HEADS, tm, HEAD_DIM), lambda bi, i: (bi, 0, i, 0))
    hkv_spec = pl.BlockSpec((1, GROUP_KV_HEADS, tm, HEAD_DIM), lambda bi, i: (bi, 0, i, 0))
    out_specs = (
        hq_spec, hq_spec, hq_spec,
        pl.BlockSpec((1, GROUP_HEADS * LANES, tm), lambda bi, i: (bi, 0, i)),
        pl.BlockSpec((1, tm, GROUP_HEADS * LANES), lambda bi, i: (bi, i, 0)),
        pl.BlockSpec((1, 1, GROUP_HEADS * V_EXT, tm), lambda bi, i: (bi, i, 0, 0)),
        pl.BlockSpec((1, GROUP_WIDTH, tm), lambda bi, i: (bi, 0, i)),
        hkv_spec,
        pl.BlockSpec((1, 1, GROUP_KV_HEADS * V_EXT, tm), lambda bi, i: (bi, i, 0, 0)),
        hq_spec, hkv_spec, hkv_spec,
    )
    in_specs = [
        pl.BlockSpec((1, tm, D_MODEL), lambda bi, i: (bi, i, 0)),
        _const_spec((1, D_MODEL)),
        _const_spec((D_MODEL, _IN_COLS_PADDED)),
        _const_spec((1, MLA_Q_RANK)),
        _const_spec((MLA_Q_RANK, GROUP_HEADS * LANES)),
        _const_spec((1, MLA_KV_RANK)),
        _const_spec((MLA_KV_RANK, GROUP_HEADS * LANES)),
        _const_spec((MLA_KV_RANK, GROUP_WIDTH)),
        _const_spec((HEAD_DIM, 1)),
        _const_spec((HEAD_DIM, 1)),
        pl.BlockSpec((3, tm, LANES), lambda bi, i: (0, i, 0)),
        pl.BlockSpec((3, tm, LANES), lambda bi, i: (0, i, 0)),
        pl.BlockSpec((HEAD_DIM, tm), lambda bi, i: (0, i)),
    ]
    return pl.pallas_call(
        _proj_kernel,
        grid=(b, nt),
        in_specs=in_specs,
        out_specs=out_specs,
        out_shape=out_shape,
        compiler_params=pltpu.CompilerParams(
            dimension_semantics=("parallel", "parallel"), vmem_limit_bytes=VMEM_LIMIT),
        name="projection",
    )(x, lw["pre_gain"], lw["w_in"], lw["q_gain"], lw["w_uq"], lw["kv_gain"], lw["w_uk"],
      lw["w_uv"], lw["cq_gain"], lw["ck_gain"], tabs["mla"], tabs["full"], tabs["axial"])


def _flash_kernel(qT_ref, k_ref, vT_ref, oT_ref, s_even, s_odd, *, tk, n_chunks, n_streams):
    tq = qT_ref.shape[2] // n_streams
    dv_ext = vT_ref.shape[2]

    def produce(s_ref, c):
        k = k_ref[0, pl.ds(pl.multiple_of(c * tk, tk), tk), :]
        maxes = []
        for st in range(n_streams):
            s = _bdot(k, qT_ref[0, :, st * tq:(st + 1) * tq])
            s_ref[st] = s
            maxes.append(jnp.max(s, axis=0, keepdims=True))
        return tuple(maxes)

    def consume(s_ref, chunk_max, c, carry):
        vT = vT_ref[0, c]
        out = []
        for st, (m, acc) in enumerate(carry):
            m_new = jnp.maximum(m, chunk_max[st])
            p = jnp.exp2(s_ref[st] - m_new).astype(BF16)
            acc = jnp.exp2(m - m_new) * acc + _bdot(vT, p)
            out.append((m_new, acc))
        return tuple(out)

    def pair(jj, state):
        carry, max_even = state
        max_odd = produce(s_odd, 2 * jj + 1)
        carry = consume(s_even, max_even, 2 * jj, carry)
        max_even = produce(s_even, 2 * jj + 2)
        return consume(s_odd, max_odd, 2 * jj + 1, carry), max_even

    def body(t, state):
        for u in range(FLASH_PAIRS_PER_TRIP):
            state = pair(t * FLASH_PAIRS_PER_TRIP + u, state)
        return state

    carry = tuple((jnp.full((1, tq), -jnp.inf, F32), jnp.zeros((dv_ext, tq), F32))
                  for _ in range(n_streams))
    max_even = produce(s_even, 0)
    n_pairs = n_chunks // 2 - 1
    n_trips = n_pairs // FLASH_PAIRS_PER_TRIP
    state = lax.fori_loop(0, n_trips, body, (carry, max_even))
    for jj in range(n_trips * FLASH_PAIRS_PER_TRIP, n_pairs):
        state = pair(jj, state)
    carry, max_even = state
    max_odd = produce(s_odd, n_chunks - 1)
    carry = consume(s_even, max_even, n_chunks - 2, carry)
    carry = consume(s_odd, max_odd, n_chunks - 1, carry)
    for st, (_, acc) in enumerate(carry):
        oT_ref[0, :, st * tq:(st + 1) * tq] = acc[:HEAD_DIM] / acc[HEAD_DIM:HEAD_DIM + 1]


def _flash(qT, k, vT, *, n_heads, n_kv, dk, k_head_major, tq, n_streams):
    b, _, s = qT.shape
    n_chunks, tk = vT.shape[1], vT.shape[3]
    assert n_chunks % 2 == 0 and s % tq == 0
    dv = HEAD_DIM
    rep = n_heads // n_kv
    if k_head_major:
        k_spec = pl.BlockSpec((None, 1, s, dk), lambda bi, h, i: (bi, h // rep, 0, 0))
    else:
        k_spec = pl.BlockSpec((1, s, dk), lambda bi, h, i: (bi, 0, h // rep))
    return pl.pallas_call(
        functools.partial(_flash_kernel, tk=tk, n_chunks=n_chunks, n_streams=n_streams),
        grid=(b, n_heads, s // tq),
        in_specs=[
            pl.BlockSpec((1, dk, tq), lambda bi, h, i: (bi, h, i)),
            k_spec,
            pl.BlockSpec((1, n_chunks, V_EXT, tk), lambda bi, h, i: (bi, 0, h // rep, 0)),
        ],
        out_specs=pl.BlockSpec((1, dv, tq), lambda bi, h, i: (bi, h, i)),
        out_shape=jax.ShapeDtypeStruct((b, n_heads * dv, s), F32),
        scratch_shapes=[pltpu.VMEM((n_streams, tk, tq // n_streams), F32)] * 2,
        compiler_params=pltpu.CompilerParams(
            dimension_semantics=("parallel", "parallel", "parallel"),
            vmem_limit_bytes=VMEM_LIMIT),
        name="dense_attention",
    )(qT, k, vT)


def _na_kernel(q_ref, kp_ref, kc_ref, kn_ref, vp_ref, vc_ref, vn_ref, bias_ref, o_ref,
               k_win, v_win, *, n_rows):
    i = pl.program_id(1)
    blk = NA_ROWS_PER_STEP * GRID_W
    band = NA_WIN_ROWS * GRID_W
    for w, (kr, vr) in enumerate(((kp_ref, vp_ref), (kc_ref, vc_ref), (kn_ref, vn_ref))):
        k_win[:, w * blk:(w + 1) * blk, :] = kr[0]
        v_win[:, w * blk:(w + 1) * blk, :] = vr[0]
    heads = range(GROUP_HEADS)
    for j0 in range(0, NA_ROWS_PER_STEP, NA_ROWS_INTERLEAVED):
        rows = range(j0, j0 + NA_ROWS_INTERLEAVED)
        offs, ds = {}, {}
        for j in rows:
            r = i * NA_ROWS_PER_STEP + j
            rs = jnp.clip(r - NA_WIN_ROWS // 2, 0, n_rows - NA_WIN_ROWS)
            offs[j] = pl.multiple_of((rs - (i - 1) * NA_ROWS_PER_STEP) * GRID_W, GRID_W)
            ds[j] = r - rs
        ss = {(j, hd): lax.dot_general(q_ref[0, hd, j * GRID_W:(j + 1) * GRID_W, :],
                                       k_win[hd, pl.ds(offs[j], band), :],
                                       (((1,), (1,)), ((), ())), preferred_element_type=F32)
              + bias_ref[hd, ds[j]] for j in rows for hd in heads}
        ps = {key: jnp.exp(s - jnp.max(s, axis=-1, keepdims=True)) for key, s in ss.items()}
        for j in rows:
            os = [_bdot(ps[j, hd].astype(BF16), v_win[hd, pl.ds(offs[j], band), :])
                  / jnp.sum(ps[j, hd], axis=-1, keepdims=True) for hd in heads]
            o_ref[0, j * GRID_W:(j + 1) * GRID_W, :] = jnp.concatenate(os, axis=1)


def _neighbourhood(q, k, v, bias):
    b, nh, s, hd = q.shape
    blk = NA_ROWS_PER_STEP * GRID_W
    nb = s // blk
    n_rows = s // GRID_W
    cur = lambda bi, i: (bi, 0, i, 0)
    prev = lambda bi, i: (bi, 0, jnp.maximum(i - 1, 0), 0)
    nxt = lambda bi, i: (bi, 0, jnp.minimum(i + 1, nb - 1), 0)
    spec = lambda f: pl.BlockSpec((1, nh, blk, hd), f)
    return pl.pallas_call(
        functools.partial(_na_kernel, n_rows=n_rows),
        grid=(b, nb),
        in_specs=[spec(cur), spec(prev), spec(cur), spec(nxt), spec(prev), spec(cur), spec(nxt),
                  _const_spec(bias.shape)],
        out_specs=pl.BlockSpec((1, blk, nh * hd), lambda bi, i: (bi, i, 0)),
        out_shape=jax.ShapeDtypeStruct((b, s, nh * hd), F32),
        scratch_shapes=[pltpu.VMEM((nh, 3 * blk, hd), BF16), pltpu.VMEM((nh, 3 * blk, hd), BF16)],
        compiler_params=pltpu.CompilerParams(
            dimension_semantics=("parallel", "parallel"), vmem_limit_bytes=VMEM_LIMIT),
        name="neighbourhood_attention",
    )(q, k, k, k, v, v, v, bias)


def _na_bias_table(rpb):
    c = np.arange(GRID_W)[:, None]
    kc = np.arange(GRID_W)[None, :]
    cs = np.clip(c - NA_WIN_COLS // 2, 0, GRID_W - NA_WIN_COLS)
    valid = (kc >= cs) & (kc < cs + NA_WIN_COLS)
    col_off = kc - c + (NA_WIN_COLS - 1)
    n_off = 2 * NA_WIN_COLS - 1
    select = (valid[:, :, None] & (col_off[:, :, None] == np.arange(n_off))).astype(np.float32)
    x = jnp.einsum("hro,cko->hrck", rpb.astype(F32), jnp.asarray(select),
                   precision=lax.Precision.HIGHEST)
    x = jnp.where(valid[None, None], x, MASK_VALUE)
    t = jnp.stack([x[:, NA_WIN_ROWS - 1 - d:2 * NA_WIN_ROWS - 1 - d] for d in range(NA_WIN_ROWS)],
                  axis=1)
    t = t.transpose(0, 1, 3, 2, 4)
    return t.reshape(rpb.shape[0], NA_WIN_ROWS, GRID_W, NA_WIN_ROWS * GRID_W)


def _sw_kernel(sink_ref, q_ref, kp_ref, kc_ref, kn_ref, vp_ref, vc_ref, vn_ref, o_ref, *, seq):
    i = pl.program_id(1)
    tq = q_ref.shape[2]
    span = tq + 2 * SW_WINDOW
    t0 = i * tq
    row = lax.broadcasted_iota(jnp.int32, (tq, span), 0)
    col = lax.broadcasted_iota(jnp.int32, (tq, span), 1)
    kpos = col + (t0 - SW_WINDOW)
    rel = col - row
    valid = (rel >= 0) & (rel <= 2 * SW_WINDOW) & (kpos >= 0) & (kpos < seq)
    nt = (((1,), (1,)), ((), ()))
    rep = GROUP_HEADS // GROUP_KV_HEADS
    heads = range(GROUP_HEADS)
    ss = [jnp.where(valid, jnp.concatenate(
        [lax.dot_general(q_ref[0, hd], kr[0, hd // rep], nt, preferred_element_type=F32)
         for kr in (kp_ref, kc_ref, kn_ref)], axis=1), MASK_VALUE) for hd in heads]
    ms = [jnp.maximum(jnp.max(ss[hd], axis=-1, keepdims=True), sink_ref[hd]) for hd in heads]
    ps = [jnp.exp(ss[hd] - ms[hd]) for hd in heads]
    denoms = [jnp.sum(ps[hd], axis=-1, keepdims=True) + jnp.exp(sink_ref[hd] - ms[hd])
              for hd in heads]
    os = []
    for hd in heads:
        g = hd // rep
        pb = ps[hd].astype(BF16)
        o = (_bdot(pb[:, :SW_WINDOW], vp_ref[0, g])
             + _bdot(pb[:, SW_WINDOW:SW_WINDOW + tq], vc_ref[0, g])
             + _bdot(pb[:, SW_WINDOW + tq:], vn_ref[0, g]))
        os.append(o / denoms[hd])
    o_ref[0] = jnp.concatenate(os, axis=1)


def _sliding_window(q, k, v, sink, tq):
    b, nh, s, hd = q.shape
    nkv = k.shape[1]
    nb = s // tq
    r = tq // SW_WINDOW
    n_small = s // SW_WINDOW
    cur = pl.BlockSpec((1, nkv, tq, hd), lambda bi, i: (bi, 0, i, 0))
    prev = pl.BlockSpec((1, nkv, SW_WINDOW, hd), lambda bi, i: (bi, 0, jnp.maximum(i * r - 1, 0), 0))
    nxt = pl.BlockSpec((1, nkv, SW_WINDOW, hd),
                       lambda bi, i: (bi, 0, jnp.minimum((i + 1) * r, n_small - 1), 0))
    return pl.pallas_call(
        functools.partial(_sw_kernel, seq=s),
        grid=(b, nb),
        in_specs=[pl.BlockSpec(memory_space=pltpu.SMEM),
                  pl.BlockSpec((1, nh, tq, hd), lambda bi, i: (bi, 0, i, 0)),
                  prev, cur, nxt, prev, cur, nxt],
        out_specs=pl.BlockSpec((1, tq, nh * hd), lambda bi, i: (bi, i, 0)),
        out_shape=jax.ShapeDtypeStruct((b, s, nh * hd), F32),
        compiler_params=pltpu.CompilerParams(
            dimension_semantics=("parallel", "parallel"), vmem_limit_bytes=VMEM_LIMIT),
        name="sliding_window_attention",
    )(sink, q, k, k, k, v, v, v)


def _out_kernel(x_ref, oa_ref, obT_ref, ocT_ref, od_ref, w_ref, g_ref, o_ref):
    mixed_in = jnp.concatenate(
        [oa_ref[0].astype(BF16), obT_ref[0].T.astype(BF16), ocT_ref[0].T.astype(BF16),
         od_ref[0].astype(BF16)], axis=1)
    mixed = _bdot(mixed_in, w_ref[...])
    o_ref[0] = x_ref[0] + _rms(mixed, g_ref[...])


def _out_projection(x, o_a, o_bT, o_cT, o_d, w_out, gain, tm):
    b, s, _ = x.shape
    tok = lambda w: pl.BlockSpec((1, tm, w), lambda bi, i: (bi, i, 0))
    feat = pl.BlockSpec((1, GROUP_WIDTH, tm), lambda bi, i: (bi, 0, i))
    return pl.pallas_call(
        _out_kernel,
        grid=(b, s // tm),
        in_specs=[tok(D_MODEL), tok(GROUP_WIDTH), feat, feat, tok(GROUP_WIDTH),
                  _const_spec((D_MODEL, D_MODEL)), _const_spec((1, D_MODEL))],
        out_specs=tok(D_MODEL),
        out_shape=jax.ShapeDtypeStruct(x.shape, F32),
        compiler_params=pltpu.CompilerParams(
            dimension_semantics=("parallel", "parallel"), vmem_limit_bytes=VMEM_LIMIT),
        name="out_projection",
    )(x, o_a, o_bT, o_cT, o_d, w_out, gain)


FFN_HALO = 8


def _ffn_kernel(x_ref, xp_ref, xn_ref, g_ref, wg_ref, wv_ref, cwg_ref, cwv_ref, cbg_ref, cbv_ref,
                wd_ref, pg_ref, o_ref, h_scr, acc_scr, u_even, u_odd, *, n_chunks):
    i = pl.program_id(1)
    n_tiles = pl.num_programs(1)
    tm = x_ref.shape[1]
    ext = tm + 2 * FFN_HALO
    g = g_ref[...]
    hp = _rms(xp_ref[0], g) * (i > 0).astype(F32)
    hn = _rms(xn_ref[0], g) * (i < n_tiles - 1).astype(F32)
    h_scr[...] = jnp.concatenate([hp, _rms(x_ref[0], g), hn], axis=0).astype(BF16)
    acc_scr[...] = jnp.zeros_like(acc_scr)

    def produce(u_ref, c):
        hh = h_scr[...]
        u_ref[0] = _bdot(hh, wg_ref[c])
        u_ref[1] = _bdot(hh, wv_ref[c])

    def conv(u_ref, cw, cb):
        lo = FFN_HALO - 1
        return (cb + u_ref[lo:lo + tm] * cw[0:1] + u_ref[lo + 1:lo + 1 + tm] * cw[1:2]
                + u_ref[lo + 2:lo + 2 + tm] * cw[2:3])

    def consume(u_ref, c):
        gate = conv(u_ref.at[0], cwg_ref[c], cbg_ref[c])
        val = conv(u_ref.at[1], cwv_ref[c], cbv_ref[c])
        act = jax.nn.gelu(gate, approximate=True) * val
        acc_scr[...] += _bdot(act.astype(BF16), wd_ref[c])

    def pair(jj):
        produce(u_odd, 2 * jj + 1)
        consume(u_even, 2 * jj)
        produce(u_even, 2 * jj + 2)
        consume(u_odd, 2 * jj + 1)

    def body(t, carry):
        for u in range(FFN_PAIRS_PER_TRIP):
            pair(t * FFN_PAIRS_PER_TRIP + u)
        return carry

    n_pairs = (n_chunks - 1) // 2
    tail = n_chunks - 2 * n_pairs
    n_trips = n_pairs // FFN_PAIRS_PER_TRIP
    produce(u_even, 0)
    lax.fori_loop(0, n_trips, body, 0)
    for jj in range(n_trips * FFN_PAIRS_PER_TRIP, n_pairs):
        pair(jj)
    if tail == 2:
        produce(u_odd, n_chunks - 1)
    consume(u_even, 2 * n_pairs)
    if tail == 2:
        consume(u_odd, n_chunks - 1)
    o_ref[0] = x_ref[0] + _rms(acc_scr[...], pg_ref[...])


def _ffn(x, lw, tm):
    b, s, _ = x.shape
    n_chunks, _, fc = lw["w_gate"].shape
    r = tm // FFN_HALO
    n_halo = s // FFN_HALO
    tile = pl.BlockSpec((1, tm, D_MODEL), lambda bi, i: (bi, i, 0))
    prev = pl.BlockSpec((1, FFN_HALO, D_MODEL), lambda bi, i: (bi, jnp.maximum(i * r - 1, 0), 0))
    nxt = pl.BlockSpec((1, FFN_HALO, D_MODEL),
                       lambda bi, i: (bi, jnp.minimum((i + 1) * r, n_halo - 1), 0))
    return pl.pallas_call(
        functools.partial(_ffn_kernel, n_chunks=n_chunks),
        grid=(b, s // tm),
        in_specs=[tile, prev, nxt, _const_spec((1, D_MODEL)),
                  _const_spec((n_chunks, D_MODEL, fc)), _const_spec((n_chunks, D_MODEL, fc)),
                  _const_spec((n_chunks, 3, fc)), _const_spec((n_chunks, 3, fc)),
                  _const_spec((n_chunks, 1, fc)), _const_spec((n_chunks, 1, fc)),
                  _const_spec((n_chunks, fc, D_MODEL)), _const_spec((1, D_MODEL))],
        out_specs=tile,
        out_shape=jax.ShapeDtypeStruct(x.shape, F32),
        scratch_shapes=[pltpu.VMEM((tm + 2 * FFN_HALO, D_MODEL), BF16),
                        pltpu.VMEM((tm, D_MODEL), F32),
                        pltpu.VMEM((2, tm + 2 * FFN_HALO, fc), F32),
                        pltpu.VMEM((2, tm + 2 * FFN_HALO, fc), F32)],
        compiler_params=pltpu.CompilerParams(
            dimension_semantics=("parallel", "parallel"), vmem_limit_bytes=VMEM_LIMIT),
        name="conv_mlp",
    )(x, x, x, lw["ffn_pre_gain"], lw["w_gate"], lw["w_val"], lw["cw_gate"], lw["cw_val"],
      lw["cb_gate"], lw["cb_val"], lw["w_down"], lw["ffn_post_gain"])


def _rope_tables(s):
    t = jnp.arange(s)

    def angles(pos, dim):
        inv = ROPE_THETA ** (-jnp.arange(0, dim, 2, dtype=F32) / dim)
        return pos.astype(F32)[:, None] * inv[None, :]

    def lane_table(ang, lead, trail, reps):
        half = ang.shape[1]
        cos, sin, zero = jnp.cos(ang), jnp.sin(ang), jnp.zeros_like(ang)
        one = lambda n: jnp.ones((s, n), F32)
        nul = lambda n: jnp.zeros((s, n), F32)
        c = jnp.concatenate([one(lead)] + [cos, cos] * reps + [one(trail)], axis=1)
        lo = jnp.concatenate([nul(lead)] + [-sin, zero] * reps + [nul(trail)], axis=1)
        hi = jnp.concatenate([nul(lead)] + [zero, sin] * reps + [nul(trail)], axis=1)
        assert c.shape[1] == LANES and 2 * half * reps + lead + trail == LANES
        return jnp.stack([c, lo, hi])

    ang_row = angles(t // GRID_W, HEAD_DIM // 2)
    ang_col = angles(t % GRID_W, HEAD_DIM // 2)
    return {
        "mla": lane_table(angles(t, MLA_ROPE), MLA_NOPE, LANES - MLA_NOPE - MLA_ROPE, 1),
        "full": lane_table(angles(t, HEAD_DIM), 0, 0, LANES // HEAD_DIM),
        "axial": jnp.concatenate([jnp.cos(ang_row), jnp.sin(ang_row),
                                  jnp.cos(ang_col), jnp.sin(ang_col)], axis=1).T,
    }


def _layer_weights(l, mix_pre_gain, w_in, na_rpb, mla_q_gain, mla_w_uq, mla_kv_gain, mla_w_ukv,
                   ax_q_gain, ax_k_gain, sw_sink, w_out, mix_post_gain, ffn_pre_gain, w_up,
                   conv_w, conv_b, w_down, ffn_post_gain):
    gw, kvw = GROUP_WIDTH, GROUP_KV_HEADS * HEAD_DIM
    sizes = (gw, gw, gw, MLA_Q_RANK, MLA_KV_RANK, MLA_ROPE, gw, kvw, kvw, gw, kvw, kvw)
    bounds = np.cumsum((0,) + sizes)
    (a_q, a_k, a_v, b_cq, b_ckv, b_kr, c_q, c_k, c_v, d_q, d_k, d_v) = [
        w_in[l][:, bounds[j]:bounds[j + 1]] for j in range(len(sizes))]
    scale = HEAD_DIM ** -0.5
    zeros = lambda n: jnp.zeros((D_MODEL, n), F32)
    kr_block = jnp.concatenate([zeros(MLA_NOPE), b_kr, zeros(LANES - MLA_NOPE - MLA_ROPE)], axis=1)
    w_in_r = jnp.concatenate([a_q * scale, a_k, a_v, b_cq, b_ckv, kr_block, c_q, c_k, c_v,
                              d_q * scale, d_k, d_v], axis=1)
    assert w_in_r.shape[1] == _IN_COLS_PADDED

    uq = mla_w_uq[l].reshape(MLA_Q_RANK, GROUP_HEADS, MLA_NOPE + MLA_ROPE)
    uq = jnp.pad(uq, ((0, 0), (0, 0), (0, LANES - MLA_NOPE - MLA_ROPE)))
    ukv = mla_w_ukv[l].reshape(MLA_KV_RANK, GROUP_HEADS, MLA_NOPE + HEAD_DIM)
    uk = jnp.pad(ukv[:, :, :MLA_NOPE], ((0, 0), (0, 0), (0, LANES - MLA_NOPE)))
    uv = ukv[:, :, MLA_NOPE:]

    n_chunks = D_FF // FF_CHUNK
    chunk_cols = lambda w: w.reshape(w.shape[0], n_chunks, FF_CHUNK).transpose(1, 0, 2)
    row = lambda v: v[None, :].astype(F32)
    return {
        "pre_gain": row(mix_pre_gain[l]),
        "w_in": w_in_r.astype(BF16),
        "q_gain": row(mla_q_gain[l]),
        "w_uq": uq.reshape(MLA_Q_RANK, GROUP_HEADS * LANES).astype(BF16),
        "kv_gain": row(mla_kv_gain[l]),
        "w_uk": uk.reshape(MLA_KV_RANK, GROUP_HEADS * LANES).astype(BF16),
        "w_uv": uv.reshape(MLA_KV_RANK, GROUP_WIDTH).astype(BF16),
        "cq_gain": (ax_q_gain[l] * (scale * LOG2E))[:, None].astype(F32),
        "ck_gain": ax_k_gain[l][:, None].astype(F32),
        "na_bias": _na_bias_table(na_rpb[l]),
        "sink": sw_sink[l].astype(F32),
        "w_out": w_out[l].astype(BF16),
        "post_gain": row(mix_post_gain[l]),
        "ffn_pre_gain": row(ffn_pre_gain[l]),
        "w_gate": chunk_cols(w_up[l][:, :D_FF]).astype(BF16),
        "w_val": chunk_cols(w_up[l][:, D_FF:]).astype(BF16),
        "cw_gate": chunk_cols(conv_w[l][:, :D_FF]),
        "cw_val": chunk_cols(conv_w[l][:, D_FF:]),
        "cb_gate": chunk_cols(conv_b[l][None, :D_FF]),
        "cb_val": chunk_cols(conv_b[l][None, D_FF:]),
        "w_down": w_down[l].reshape(n_chunks, FF_CHUNK, D_MODEL).astype(BF16),
        "ffn_post_gain": row(ffn_post_gain[l]),
    }


def kernel(x, mix_pre_gain, w_in, na_rpb, mla_q_gain, mla_w_uq, mla_kv_gain, mla_w_ukv, ax_q_gain,
           ax_k_gain, sw_sink, w_out, mix_post_gain, ffn_pre_gain, w_up, conv_w, conv_b, w_down,
           ffn_post_gain):
    b, s, d = x.shape
    assert d == D_MODEL and s % max(TOKEN_TILE, FLASH_TQ, NA_ROWS_PER_STEP * GRID_W, SW_TQ) == 0
    assert s // GRID_W >= NA_WIN_ROWS
    params = (mix_pre_gain, w_in, na_rpb, mla_q_gain, mla_w_uq, mla_kv_gain, mla_w_ukv, ax_q_gain,
              ax_k_gain, sw_sink, w_out, mix_post_gain, ffn_pre_gain, w_up, conv_w, conv_b, w_down,
              ffn_post_gain)
    tabs = _rope_tables(s)
    for l in range(w_in.shape[0]):
        lw = _layer_weights(l, *params)
        (a_q, a_k, a_v, b_qT, b_k, b_vT, c_qT, c_k, c_vT, d_q, d_k, d_v) = _projection(
            x, lw, tabs, TOKEN_TILE)
        o_a = _neighbourhood(a_q, a_k, a_v, lw["na_bias"])
        o_bT = _flash(b_qT, b_k, b_vT, n_heads=GROUP_HEADS, n_kv=GROUP_HEADS, dk=LANES,
                      k_head_major=False, tq=FLASH_TQ * FLASH_STREAMS, n_streams=FLASH_STREAMS)
        o_cT = _flash(c_qT, c_k, c_vT, n_heads=GROUP_HEADS, n_kv=GROUP_KV_HEADS, dk=HEAD_DIM,
                      k_head_major=True, tq=FLASH_TQ * FLASH_STREAMS, n_streams=FLASH_STREAMS)
        o_d = _sliding_window(d_q, d_k, d_v, lw["sink"], SW_TQ)
        x = _out_projection(x, o_a, o_bT, o_cT, o_d, lw["w_out"], lw["post_gain"], TOKEN_TILE)
        x = _ffn(x, lw, TOKEN_TILE)
    return x
```

```python
import functools
import math

import numpy as np
import jax
import jax.numpy as jnp
from jax import lax
from jax.experimental import pallas as pl
from jax.experimental.pallas import tpu as pltpu

F32 = jnp.float32
BF16 = jnp.bfloat16

D_MODEL = 1024
GRID_W = 64
HEAD_DIM = 64
GROUP_HEADS = 4
GROUP_KV_HEADS = 2
GROUP_WIDTH = GROUP_HEADS * HEAD_DIM
ROPE_THETA = 10000.0
NORM_EPS = 1e-6
MASK_VALUE = -1e30
NA_WIN_ROWS = 8
NA_WIN_COLS = 16
MLA_Q_RANK = 256
MLA_KV_RANK = 128
MLA_NOPE = 64
MLA_ROPE = 32
SW_WINDOW = 128
D_FF = 2816
LOG2E = math.log2(math.e)

LANES = 128
VMEM_LIMIT = 56 * 1024 * 1024

TOKEN_TILE = 512
FLASH_TQ = 512
FLASH_TK = 256
FLASH_STREAMS = 2
FLASH_PAIRS_PER_TRIP = 5
V_EXT = HEAD_DIM + 16
NA_ROWS_PER_STEP = 8
NA_ROWS_INTERLEAVED = 4
SW_TQ = 256
FFN_TILE = 1024
FFN_PAIRS_PER_TRIP = 1
FF_CHUNK = 256


def _rms(x, gain):
    return x * lax.rsqrt(jnp.mean(x * x, axis=-1, keepdims=True) + NORM_EPS) * gain


def _rope_lanes(x, tab_ref, half):
    w = x.shape[-1]
    return (x * tab_ref[0] + pltpu.roll(x, w - half, 1) * tab_ref[1]
            + pltpu.roll(x, half, 1) * tab_ref[2])


def _bdot(a, b):
    return jnp.dot(a, b, preferred_element_type=F32)


_C_AQ, _C_AK, _C_AV = 0, 256, 512
_C_BCQ, _C_BCKV, _C_BKR = 768, 1024, 1152
_C_CQ, _C_CK, _C_CV = 1280, 1536, 1664
_C_DQ, _C_DK, _C_DV = 1792, 2048, 2176
_IN_COLS_PADDED = 2304
_PAIR = 2 * LANES


def _proj_kernel(x_ref, g_ref, win_ref, qg_ref, wuq_ref, kvg_ref, wuk_ref, wuv_ref,
                 cqg_ref, ckg_ref, tabb_ref, tabd_ref, tabc_ref,
                 aq_ref, ak_ref, av_ref, bqT_ref, bk_ref, bvT_ref,
                 cqT_ref, ck_ref, cvT_ref, dq_ref, dk_ref, dv_ref):
    h = _rms(x_ref[0], g_ref[...]).astype(BF16)

    def proj(c0, width):
        return _bdot(h, win_ref[:, c0:c0 + width])

    def store_heads(ref, z, n_heads):
        for hd in range(n_heads):
            ref[0, hd] = z[:, hd * HEAD_DIM:(hd + 1) * HEAD_DIM].astype(BF16)

    def store_vT_ext(ref, vT, n_heads):
        tk = ref.shape[3]
        pad = V_EXT - HEAD_DIM
        ones_row = (lax.broadcasted_iota(jnp.int32, (pad, tk), 0) == 0).astype(F32).astype(BF16)
        for t in range(ref.shape[1]):
            for hd in range(n_heads):
                ref[0, t, hd * V_EXT:hd * V_EXT + HEAD_DIM, :] = (
                    vT[hd * HEAD_DIM:(hd + 1) * HEAD_DIM, t * tk:(t + 1) * tk].astype(BF16))
                ref[0, t, hd * V_EXT + HEAD_DIM:(hd + 1) * V_EXT, :] = ones_row

    store_heads(aq_ref, proj(_C_AQ, GROUP_WIDTH), GROUP_HEADS)
    store_heads(ak_ref, proj(_C_AK, GROUP_WIDTH), GROUP_HEADS)
    store_heads(av_ref, proj(_C_AV, GROUP_WIDTH), GROUP_HEADS)

    cq = _rms(proj(_C_BCQ, MLA_Q_RANK), qg_ref[...]).astype(BF16)
    qb = _bdot(cq, wuq_ref[...]) * ((MLA_NOPE + MLA_ROPE) ** -0.5 * LOG2E)
    assert (_C_BKR, _C_CV, _C_DV) == (_C_BCKV + LANES, _C_CK + LANES, _C_DK + LANES)
    b_pair = proj(_C_BCKV, _PAIR)
    kpe = _rope_lanes(b_pair[:, LANES:], tabb_ref, MLA_ROPE // 2)
    ckv = _rms(b_pair[:, :LANES], kvg_ref[...]).astype(BF16)
    kn = _bdot(ckv, wuk_ref[...])
    for hd in range(GROUP_HEADS):
        blk = slice(hd * LANES, (hd + 1) * LANES)
        qh = _rope_lanes(qb[:, blk], tabb_ref, MLA_ROPE // 2)
        bqT_ref[0, blk, :] = qh.T.astype(BF16)
        bk_ref[0, :, blk] = (kn[:, blk] + kpe).astype(BF16)
    store_vT_ext(bvT_ref, _bdot(ckv, wuv_ref[...]).T, GROUP_HEADS)

    def norm_rope_T(blk, gain_col):
        ms = jnp.mean(blk * blk, axis=0, keepdims=True)
        blk = blk * lax.rsqrt(ms + NORM_EPS) * gain_col
        q = HEAD_DIM // 4
        cr, sr = tabc_ref[0:q], tabc_ref[q:2 * q]
        cc, sc = tabc_ref[2 * q:3 * q], tabc_ref[3 * q:4 * q]
        x1, x2, x3, x4 = blk[0:q], blk[q:2 * q], blk[2 * q:3 * q], blk[3 * q:4 * q]
        return jnp.concatenate([x1 * cr - x2 * sr, x2 * cr + x1 * sr,
                                x3 * cc - x4 * sc, x4 * cc + x3 * sc], axis=0)

    cqT = proj(_C_CQ, GROUP_WIDTH).T
    for hd in range(GROUP_HEADS):
        rows = slice(hd * HEAD_DIM, (hd + 1) * HEAD_DIM)
        cqT_ref[0, rows, :] = norm_rope_T(cqT[rows], cqg_ref[...]).astype(BF16)
    c_pair = proj(_C_CK, _PAIR)
    ckT = c_pair[:, :LANES].T
    ck = jnp.concatenate(
        [norm_rope_T(ckT[hd * HEAD_DIM:(hd + 1) * HEAD_DIM], ckg_ref[...])
         for hd in range(GROUP_KV_HEADS)], axis=0).T
    store_heads(ck_ref, ck, GROUP_KV_HEADS)
    store_vT_ext(cvT_ref, c_pair[:, LANES:].T, GROUP_KV_HEADS)

    dq = proj(_C_DQ, GROUP_WIDTH)
    dq = jnp.concatenate([_rope_lanes(dq[:, j * LANES:(j + 1) * LANES], tabd_ref, HEAD_DIM // 2)
                          for j in range(GROUP_WIDTH // LANES)], axis=1)
    store_heads(dq_ref, dq, GROUP_HEADS)
    d_pair = proj(_C_DK, _PAIR)
    dk = _rope_lanes(d_pair[:, :LANES], tabd_ref, HEAD_DIM // 2)
    store_heads(dk_ref, dk, GROUP_KV_HEADS)
    store_heads(dv_ref, d_pair[:, LANES:], GROUP_KV_HEADS)


def _const_spec(shape):
    n = len(shape)
    return pl.BlockSpec(shape, lambda *_: (0,) * n, pipeline_mode=pl.Buffered(1))


def _projection(x, lw, tabs, tm):
    b, s, _ = x.shape
    nt = s // tm
    nk = s // FLASH_TK
    assert tm % FLASH_TK == 0
    head_q = jax.ShapeDtypeStruct((b, GROUP_HEADS, s, HEAD_DIM), BF16)
    head_kv = jax.ShapeDtypeStruct((b, GROUP_KV_HEADS, s, HEAD_DIM), BF16)
    out_shape = (
        head_q, head_q, head_q,
        jax.ShapeDtypeStruct((b, GROUP_HEADS * LANES, s), BF16),
        jax.ShapeDtypeStruct((b, s, GROUP_HEADS * LANES), BF16),
        jax.ShapeDtypeStruct((b, nk, GROUP_HEADS * V_EXT, FLASH_TK), BF16),
        jax.ShapeDtypeStruct((b, GROUP_WIDTH, s), BF16),
        head_kv,
        jax.ShapeDtypeStruct((b, nk, GROUP_KV_HEADS * V_EXT, FLASH_TK), BF16),
        head_q, head_kv, head_kv,
    )
    hq_spec = pl.BlockSpec((1, GROUP_HEADS, tm, HEAD_DIM), lambda bi, i: (bi, 0, i, 0))
    hkv_spec = pl.BlockSpec((1, GROUP_KV_HEADS, tm, HEAD_DIM), lambda bi, i: (bi, 0, i, 0))
    out_specs = (
        hq_spec, hq_spec, hq_spec,
        pl.BlockSpec((1, GROUP_HEADS * LANES, tm), lambda bi, i: (bi, 0, i)),
        pl.BlockSpec((1, tm, GROUP_HEADS * LANES), lambda bi, i: (bi, i, 0)),
        pl.BlockSpec((1, tm // FLASH_TK, GROUP_HEADS * V_EXT, FLASH_TK),
                     lambda bi, i: (bi, i, 0, 0)),
        pl.BlockSpec((1, GROUP_WIDTH, tm), lambda bi, i: (bi, 0, i)),
        hkv_spec,
        pl.BlockSpec((1, tm // FLASH_TK, GROUP_KV_HEADS * V_EXT, FLASH_TK),
                     lambda bi, i: (bi, i, 0, 0)),
        hq_spec, hkv_spec, hkv_spec,
    )
    in_specs = [
        pl.BlockSpec((1, tm, D_MODEL), lambda bi, i: (bi, i, 0)),
        _const_spec((1, D_MODEL)),
        _const_spec((D_MODEL, _IN_COLS_PADDED)),
        _const_spec((1, MLA_Q_RANK)),
        _const_spec((MLA_Q_RANK, GROUP_HEADS * LANES)),
        _const_spec((1, MLA_KV_RANK)),
        _const_spec((MLA_KV_RANK, GROUP_HEADS * LANES)),
        _const_spec((MLA_KV_RANK, GROUP_WIDTH)),
        _const_spec((HEAD_DIM, 1)),
        _const_spec((HEAD_DIM, 1)),
        pl.BlockSpec((3, tm, LANES), lambda bi, i: (0, i, 0)),
        pl.BlockSpec((3, tm, LANES), lambda bi, i: (0, i, 0)),
        pl.BlockSpec((HEAD_DIM, tm), lambda bi, i: (0, i)),
    ]
    return pl.pallas_call(
        _proj_kernel,
        grid=(b, nt),
        in_specs=in_specs,
        out_specs=out_specs,
        out_shape=out_shape,
        compiler_params=pltpu.CompilerParams(
            dimension_semantics=("parallel", "parallel"), vmem_limit_bytes=VMEM_LIMIT),
        name="projection",
    )(x, lw["pre_gain"], lw["w_in"], lw["q_gain"], lw["w_uq"], lw["kv_gain"], lw["w_uk"],
      lw["w_uv"], lw["cq_gain"], lw["ck_gain"], tabs["mla"], tabs["full"], tabs["axial"])


def _flash_kernel(qT_ref, k_ref, vT_ref, oT_ref, s_even, s_odd, *, tk, n_chunks, n_streams):
    tq = qT_ref.shape[2] // n_streams
    dv_ext = vT_ref.shape[2]

    def produce(s_ref, c):
        k = k_ref[0, pl.ds(pl.multiple_of(c * tk, tk), tk), :]
        maxes = []
        for st in range(n_streams):
            s = _bdot(k, qT_ref[0, :, st * tq:(st + 1) * tq])
            s_ref[st] = s
            maxes.append(jnp.max(s, axis=0, keepdims=True))
        return tuple(maxes)

    def consume(s_ref, chunk_max, c, carry):
        vT = vT_ref[0, c]
        out = []
        for st, (m, acc) in enumerate(carry):
            m_new = jnp.maximum(m, chunk_max[st])
            p = jnp.exp2(s_ref[st] - m_new).astype(BF16)
            acc = jnp.exp2(m - m_new) * acc + _bdot(vT, p)
            out.append((m_new, acc))
        return tuple(out)

    def pair(jj, state):
        carry, max_even = state
        max_odd = produce(s_odd, 2 * jj + 1)
        carry = consume(s_even, max_even, 2 * jj, carry)
        max_even = produce(s_even, 2 * jj + 2)
        return consume(s_odd, max_odd, 2 * jj + 1, carry), max_even

    def body(t, state):
        for u in range(FLASH_PAIRS_PER_TRIP):
            state = pair(t * FLASH_PAIRS_PER_TRIP + u, state)
        return state

    carry = tuple((jnp.full((1, tq), -jnp.inf, F32), jnp.zeros((dv_ext, tq), F32))
                  for _ in range(n_streams))
    max_even = produce(s_even, 0)
    n_pairs = n_chunks // 2 - 1
    n_trips = n_pairs // FLASH_PAIRS_PER_TRIP
    state = lax.fori_loop(0, n_trips, body, (carry, max_even))
    for jj in range(n_trips * FLASH_PAIRS_PER_TRIP, n_pairs):
        state = pair(jj, state)
    carry, max_even = state
    max_odd = produce(s_odd, n_chunks - 1)
    carry = consume(s_even, max_even, n_chunks - 2, carry)
    carry = consume(s_odd, max_odd, n_chunks - 1, carry)
    for st, (_, acc) in enumerate(carry):
        oT_ref[0, :, st * tq:(st + 1) * tq] = acc[:HEAD_DIM] / acc[HEAD_DIM:HEAD_DIM + 1]


def _flash(qT, k, vT, *, n_heads, n_kv, dk, k_head_major, tq, n_streams):
    b, _, s = qT.shape
    n_chunks, tk = vT.shape[1], vT.shape[3]
    assert n_chunks % 2 == 0 and s % tq == 0
    dv = HEAD_DIM
    rep = n_heads // n_kv
    if k_head_major:
        k_spec = pl.BlockSpec((None, 1, s, dk), lambda bi, h, i: (bi, h // rep, 0, 0))
    else:
        k_spec = pl.BlockSpec((1, s, dk), lambda bi, h, i: (bi, 0, h // rep))
    return pl.pallas_call(
        functools.partial(_flash_kernel, tk=tk, n_chunks=n_chunks, n_streams=n_streams),
        grid=(b, n_heads, s // tq),
        in_specs=[
            pl.BlockSpec((1, dk, tq), lambda bi, h, i: (bi, h, i)),
            k_spec,
            pl.BlockSpec((1, n_chunks, V_EXT, tk), lambda bi, h, i: (bi, 0, h // rep, 0)),
        ],
        out_specs=pl.BlockSpec((1, dv, tq), lambda bi, h, i: (bi, h, i)),
        out_shape=jax.ShapeDtypeStruct((b, n_heads * dv, s), F32),
        scratch_shapes=[pltpu.VMEM((n_streams, tk, tq // n_streams), F32)] * 2,
        compiler_params=pltpu.CompilerParams(
            dimension_semantics=("parallel", "parallel", "parallel"),
            vmem_limit_bytes=VMEM_LIMIT),
        name="dense_attention",
    )(qT, k, vT)


def _na_kernel(q_ref, kp_ref, kc_ref, kn_ref, vp_ref, vc_ref, vn_ref, bias_ref, o_ref,
               k_win, v_win, *, n_rows):
    i = pl.program_id(1)
    blk = NA_ROWS_PER_STEP * GRID_W
    band = NA_WIN_ROWS * GRID_W
    for w, (kr, vr) in enumerate(((kp_ref, vp_ref), (kc_ref, vc_ref), (kn_ref, vn_ref))):
        k_win[:, w * blk:(w + 1) * blk, :] = kr[0]
        v_win[:, w * blk:(w + 1) * blk, :] = vr[0]
    heads = range(GROUP_HEADS)
    for j0 in range(0, NA_ROWS_PER_STEP, NA_ROWS_INTERLEAVED):
        rows = range(j0, j0 + NA_ROWS_INTERLEAVED)
        offs, ds = {}, {}
        for j in rows:
            r = i * NA_ROWS_PER_STEP + j
            rs = jnp.clip(r - NA_WIN_ROWS // 2, 0, n_rows - NA_WIN_ROWS)
            offs[j] = pl.multiple_of((rs - (i - 1) * NA_ROWS_PER_STEP) * GRID_W, GRID_W)
            ds[j] = r - rs
        ss = {(j, hd): lax.dot_general(q_ref[0, hd, j * GRID_W:(j + 1) * GRID_W, :],
                                       k_win[hd, pl.ds(offs[j], band), :],
                                       (((1,), (1,)), ((), ())), preferred_element_type=F32)
              + bias_ref[hd, ds[j]] for j in rows for hd in heads}
        ps = {key: jnp.exp(s - jnp.max(s, axis=-1, keepdims=True)) for key, s in ss.items()}
        for j in rows:
            os = [_bdot(ps[j, hd].astype(BF16), v_win[hd, pl.ds(offs[j], band), :])
                  / jnp.sum(ps[j, hd], axis=-1, keepdims=True) for hd in heads]
            o_ref[0, j * GRID_W:(j + 1) * GRID_W, :] = jnp.concatenate(os, axis=1)


def _neighbourhood(q, k, v, bias):
    b, nh, s, hd = q.shape
    blk = NA_ROWS_PER_STEP * GRID_W
    nb = s // blk
    n_rows = s // GRID_W
    cur = lambda bi, i: (bi, 0, i, 0)
    prev = lambda bi, i: (bi, 0, jnp.maximum(i - 1, 0), 0)
    nxt = lambda bi, i: (bi, 0, jnp.minimum(i + 1, nb - 1), 0)
    spec = lambda f: pl.BlockSpec((1, nh, blk, hd), f)
    return pl.pallas_call(
        functools.partial(_na_kernel, n_rows=n_rows),
        grid=(b, nb),
        in_specs=[spec(cur), spec(prev), spec(cur), spec(nxt), spec(prev), spec(cur), spec(nxt),
                  _const_spec(bias.shape)],
        out_specs=pl.BlockSpec((1, blk, nh * hd), lambda bi, i: (bi, i, 0)),
        out_shape=jax.ShapeDtypeStruct((b, s, nh * hd), F32),
        scratch_shapes=[pltpu.VMEM((nh, 3 * blk, hd), BF16), pltpu.VMEM((nh, 3 * blk, hd), BF16)],
        compiler_params=pltpu.CompilerParams(
            dimension_semantics=("parallel", "parallel"), vmem_limit_bytes=VMEM_LIMIT),
        name="neighbourhood_attention",
    )(q, k, k, k, v, v, v, bias)


def _na_bias_table(rpb):
    c = np.arange(GRID_W)[:, None]
    kc = np.arange(GRID_W)[None, :]
    cs = np.clip(c - NA_WIN_COLS // 2, 0, GRID_W - NA_WIN_COLS)
    valid = (kc >= cs) & (kc < cs + NA_WIN_COLS)
    col_off = kc - c + (NA_WIN_COLS - 1)
    n_off = 2 * NA_WIN_COLS - 1
    select = (valid[:, :, None] & (col_off[:, :, None] == np.arange(n_off))).astype(np.float32)
    x = jnp.einsum("hro,cko->hrck", rpb.astype(F32), jnp.asarray(select),
                   precision=lax.Precision.HIGHEST)
    x = jnp.where(valid[None, None], x, MASK_VALUE)
    t = jnp.stack([x[:, NA_WIN_ROWS - 1 - d:2 * NA_WIN_ROWS - 1 - d] for d in range(NA_WIN_ROWS)],
                  axis=1)
    t = t.transpose(0, 1, 3, 2, 4)
    return t.reshape(rpb.shape[0], NA_WIN_ROWS, GRID_W, NA_WIN_ROWS * GRID_W)


def _sw_kernel(sink_ref, q_ref, kp_ref, kc_ref, kn_ref, vp_ref, vc_ref, vn_ref, o_ref, *, seq):
    i = pl.program_id(1)
    tq = q_ref.shape[2]
    span = tq + 2 * SW_WINDOW
    t0 = i * tq
    row = lax.broadcasted_iota(jnp.int32, (tq, span), 0)
    col = lax.broadcasted_iota(jnp.int32, (tq, span), 1)
    kpos = col + (t0 - SW_WINDOW)
    rel = col - row
    valid = (rel >= 0) & (rel <= 2 * SW_WINDOW) & (kpos >= 0) & (kpos < seq)
    nt = (((1,), (1,)), ((), ()))
    rep = GROUP_HEADS // GROUP_KV_HEADS
    heads = range(GROUP_HEADS)
    ss = [jnp.where(valid, jnp.concatenate(
        [lax.dot_general(q_ref[0, hd], kr[0, hd // rep], nt, preferred_element_type=F32)
         for kr in (kp_ref, kc_ref, kn_ref)], axis=1), MASK_VALUE) for hd in heads]
    ms = [jnp.maximum(jnp.max(ss[hd], axis=-1, keepdims=True), sink_ref[hd]) for hd in heads]
    ps = [jnp.exp(ss[hd] - ms[hd]) for hd in heads]
    denoms = [jnp.sum(ps[hd], axis=-1, keepdims=True) + jnp.exp(sink_ref[hd] - ms[hd])
              for hd in heads]
    os = []
    for hd in heads:
        g = hd // rep
        pb = ps[hd].astype(BF16)
        o = (_bdot(pb[:, :SW_WINDOW], vp_ref[0, g])
             + _bdot(pb[:, SW_WINDOW:SW_WINDOW + tq], vc_ref[0, g])
             + _bdot(pb[:, SW_WINDOW + tq:], vn_ref[0, g]))
        os.append(o / denoms[hd])
    o_ref[0] = jnp.concatenate(os, axis=1)


def _sliding_window(q, k, v, sink, tq):
    b, nh, s, hd = q.shape
    nkv = k.shape[1]
    nb = s // tq
    r = tq // SW_WINDOW
    n_small = s // SW_WINDOW
    cur = pl.BlockSpec((1, nkv, tq, hd), lambda bi, i: (bi, 0, i, 0))
    prev = pl.BlockSpec((1, nkv, SW_WINDOW, hd), lambda bi, i: (bi, 0, jnp.maximum(i * r - 1, 0), 0))
    nxt = pl.BlockSpec((1, nkv, SW_WINDOW, hd),
                       lambda bi, i: (bi, 0, jnp.minimum((i + 1) * r, n_small - 1), 0))
    return pl.pallas_call(
        functools.partial(_sw_kernel, seq=s),
        grid=(b, nb),
        in_specs=[pl.BlockSpec(memory_space=pltpu.SMEM),
                  pl.BlockSpec((1, nh, tq, hd), lambda bi, i: (bi, 0, i, 0)),
                  prev, cur, nxt, prev, cur, nxt],
        out_specs=pl.BlockSpec((1, tq, nh * hd), lambda bi, i: (bi, i, 0)),
        out_shape=jax.ShapeDtypeStruct((b, s, nh * hd), F32),
        compiler_params=pltpu.CompilerParams(
            dimension_semantics=("parallel", "parallel"), vmem_limit_bytes=VMEM_LIMIT),
        name="sliding_window_attention",
    )(sink, q, k, k, k, v, v, v)


def _out_kernel(x_ref, oa_ref, obT_ref, ocT_ref, od_ref, w_ref, g_ref, o_ref):
    mixed_in = jnp.concatenate(
        [oa_ref[0].astype(BF16), obT_ref[0].T.astype(BF16), ocT_ref[0].T.astype(BF16),
         od_ref[0].astype(BF16)], axis=1)
    mixed = _bdot(mixed_in, w_ref[...])
    o_ref[0] = x_ref[0] + _rms(mixed, g_ref[...])


def _out_projection(x, o_a, o_bT, o_cT, o_d, w_out, gain, tm):
    b, s, _ = x.shape
    tok = lambda w: pl.BlockSpec((1, tm, w), lambda bi, i: (bi, i, 0))
    feat = pl.BlockSpec((1, GROUP_WIDTH, tm), lambda bi, i: (bi, 0, i))
    return pl.pallas_call(
        _out_kernel,
        grid=(b, s // tm),
        in_specs=[tok(D_MODEL), tok(GROUP_WIDTH), feat, feat, tok(GROUP_WIDTH),
                  _const_spec((D_MODEL, D_MODEL)), _const_spec((1, D_MODEL))],
        out_specs=tok(D_MODEL),
        out_shape=jax.ShapeDtypeStruct(x.shape, F32),
        compiler_params=pltpu.CompilerParams(
            dimension_semantics=("parallel", "parallel"), vmem_limit_bytes=VMEM_LIMIT),
        name="out_projection",
    )(x, o_a, o_bT, o_cT, o_d, w_out, gain)


FFN_HALO = 8


def _ffn_kernel(x_ref, xp_ref, xn_ref, g_ref, wup_ref, cw_ref, cb_ref, wd_ref, pg_ref, o_ref,
                h_scr, acc_scr, u_even, u_odd, *, fc):
    n_chunks = D_FF // fc

    def cols(ref, c, base):
        return ref[:, pl.ds(pl.multiple_of(base + c * fc, LANES), fc)]

    i = pl.program_id(1)
    n_tiles = pl.num_programs(1)
    tm = x_ref.shape[1]
    ext = tm + 2 * FFN_HALO
    g = g_ref[...]
    hp = _rms(xp_ref[0], g) * (i > 0).astype(F32)
    hn = _rms(xn_ref[0], g) * (i < n_tiles - 1).astype(F32)
    h_scr[...] = jnp.concatenate([hp, _rms(x_ref[0], g), hn], axis=0).astype(BF16)
    acc_scr[...] = jnp.zeros_like(acc_scr)

    def produce(u_ref, c):
        hh = h_scr[...]
        u_ref[0] = _bdot(hh, cols(wup_ref, c, 0))
        u_ref[1] = _bdot(hh, cols(wup_ref, c, D_FF))

    def conv(u_ref, cw, cb):
        lo = FFN_HALO - 1
        return (cb + u_ref[lo:lo + tm] * cw[0:1] + u_ref[lo + 1:lo + 1 + tm] * cw[1:2]
                + u_ref[lo + 2:lo + 2 + tm] * cw[2:3])

    def consume(u_ref, c):
        gate = conv(u_ref.at[0], cols(cw_ref, c, 0), cols(cb_ref, c, 0))
        val = conv(u_ref.at[1], cols(cw_ref, c, D_FF), cols(cb_ref, c, D_FF))
        act = jax.nn.gelu(gate, approximate=True) * val
        acc_scr[...] += _bdot(act.astype(BF16), wd_ref[pl.ds(pl.multiple_of(c * fc, fc), fc), :])

    def pair(jj):
        produce(u_odd, 2 * jj + 1)
        consume(u_even, 2 * jj)
        produce(u_even, 2 * jj + 2)
        consume(u_odd, 2 * jj + 1)

    def body(t, carry):
        for u in range(FFN_PAIRS_PER_TRIP):
            pair(t * FFN_PAIRS_PER_TRIP + u)
        return carry

    n_pairs = (n_chunks - 1) // 2
    tail = n_chunks - 2 * n_pairs
    n_trips = n_pairs // FFN_PAIRS_PER_TRIP
    produce(u_even, 0)
    lax.fori_loop(0, n_trips, body, 0)
    for jj in range(n_trips * FFN_PAIRS_PER_TRIP, n_pairs):
        pair(jj)
    if tail == 2:
        produce(u_odd, n_chunks - 1)
    consume(u_even, 2 * n_pairs)
    if tail == 2:
        consume(u_odd, n_chunks - 1)
    o_ref[0] = x_ref[0] + _rms(acc_scr[...], pg_ref[...])


def _layer_spec(stacked, l):
    shape = stacked.shape[1:]
    return pl.BlockSpec((None,) + shape, lambda *_: (l,) + (0,) * len(shape),
                        pipeline_mode=pl.Buffered(1))


def _ffn(x, lw, mlp, l, tm):
    b, s, _ = x.shape
    fc = FF_CHUNK
    assert D_FF % fc == 0 and fc % LANES == 0
    r = tm // FFN_HALO
    n_halo = s // FFN_HALO
    tile = pl.BlockSpec((1, tm, D_MODEL), lambda bi, i: (bi, i, 0))
    prev = pl.BlockSpec((1, FFN_HALO, D_MODEL), lambda bi, i: (bi, jnp.maximum(i * r - 1, 0), 0))
    nxt = pl.BlockSpec((1, FFN_HALO, D_MODEL),
                       lambda bi, i: (bi, jnp.minimum((i + 1) * r, n_halo - 1), 0))
    return pl.pallas_call(
        functools.partial(_ffn_kernel, fc=fc),
        grid=(b, s // tm),
        in_specs=[tile, prev, nxt, _const_spec((1, D_MODEL)),
                  _layer_spec(mlp["w_up"], l), _layer_spec(mlp["conv_w"], l),
                  _layer_spec(mlp["conv_b"], l), _layer_spec(mlp["w_down"], l),
                  _const_spec((1, D_MODEL))],
        out_specs=tile,
        out_shape=jax.ShapeDtypeStruct(x.shape, F32),
        scratch_shapes=[pltpu.VMEM((tm + 2 * FFN_HALO, D_MODEL), BF16),
                        pltpu.VMEM((tm, D_MODEL), F32),
                        pltpu.VMEM((2, tm + 2 * FFN_HALO, fc), F32),
                        pltpu.VMEM((2, tm + 2 * FFN_HALO, fc), F32)],
        compiler_params=pltpu.CompilerParams(
            dimension_semantics=("parallel", "parallel"), vmem_limit_bytes=VMEM_LIMIT),
        name="conv_mlp",
    )(x, x, x, lw["ffn_pre_gain"], mlp["w_up"], mlp["conv_w"], mlp["conv_b"], mlp["w_down"],
      lw["ffn_post_gain"])


def _rope_tables(s):
    t = jnp.arange(s)

    def angles(pos, dim):
        inv = ROPE_THETA ** (-jnp.arange(0, dim, 2, dtype=F32) / dim)
        return pos.astype(F32)[:, None] * inv[None, :]

    def lane_table(ang, lead, trail, reps):
        half = ang.shape[1]
        cos, sin, zero = jnp.cos(ang), jnp.sin(ang), jnp.zeros_like(ang)
        one = lambda n: jnp.ones((s, n), F32)
        nul = lambda n: jnp.zeros((s, n), F32)
        c = jnp.concatenate([one(lead)] + [cos, cos] * reps + [one(trail)], axis=1)
        lo = jnp.concatenate([nul(lead)] + [-sin, zero] * reps + [nul(trail)], axis=1)
        hi = jnp.concatenate([nul(lead)] + [zero, sin] * reps + [nul(trail)], axis=1)
        assert c.shape[1] == LANES and 2 * half * reps + lead + trail == LANES
        return jnp.stack([c, lo, hi])

    ang_row = angles(t // GRID_W, HEAD_DIM // 2)
    ang_col = angles(t % GRID_W, HEAD_DIM // 2)
    return {
        "mla": lane_table(angles(t, MLA_ROPE), MLA_NOPE, LANES - MLA_NOPE - MLA_ROPE, 1),
        "full": lane_table(angles(t, HEAD_DIM), 0, 0, LANES // HEAD_DIM),
        "axial": jnp.concatenate([jnp.cos(ang_row), jnp.sin(ang_row),
                                  jnp.cos(ang_col), jnp.sin(ang_col)], axis=1).T,
    }


def _layer_weights(l, mix_pre_gain, w_in, na_rpb, mla_q_gain, mla_w_uq, mla_kv_gain, mla_w_ukv,
                   ax_q_gain, ax_k_gain, sw_sink, w_out, mix_post_gain, ffn_pre_gain, w_up,
                   conv_w, conv_b, w_down, ffn_post_gain):
    gw, kvw = GROUP_WIDTH, GROUP_KV_HEADS * HEAD_DIM
    sizes = (gw, gw, gw, MLA_Q_RANK, MLA_KV_RANK, MLA_ROPE, gw, kvw, kvw, gw, kvw, kvw)
    bounds = np.cumsum((0,) + sizes)
    (a_q, a_k, a_v, b_cq, b_ckv, b_kr, c_q, c_k, c_v, d_q, d_k, d_v) = [
        w_in[l][:, bounds[j]:bounds[j + 1]] for j in range(len(sizes))]
    scale = HEAD_DIM ** -0.5
    zeros = lambda n: jnp.zeros((D_MODEL, n), F32)
    kr_block = jnp.concatenate([zeros(MLA_NOPE), b_kr, zeros(LANES - MLA_NOPE - MLA_ROPE)], axis=1)
    w_in_r = jnp.concatenate([a_q * scale, a_k, a_v, b_cq, b_ckv, kr_block, c_q, c_k, c_v,
                              d_q * scale, d_k, d_v], axis=1)
    assert w_in_r.shape[1] == _IN_COLS_PADDED

    uq = mla_w_uq[l].reshape(MLA_Q_RANK, GROUP_HEADS, MLA_NOPE + MLA_ROPE)
    uq = jnp.pad(uq, ((0, 0), (0, 0), (0, LANES - MLA_NOPE - MLA_ROPE)))
    ukv = mla_w_ukv[l].reshape(MLA_KV_RANK, GROUP_HEADS, MLA_NOPE + HEAD_DIM)
    uk = jnp.pad(ukv[:, :, :MLA_NOPE], ((0, 0), (0, 0), (0, LANES - MLA_NOPE)))
    uv = ukv[:, :, MLA_NOPE:]

    row = lambda v: v[None, :].astype(F32)
    return {
        "pre_gain": row(mix_pre_gain[l]),
        "w_in": w_in_r.astype(BF16),
        "q_gain": row(mla_q_gain[l]),
        "w_uq": uq.reshape(MLA_Q_RANK, GROUP_HEADS * LANES).astype(BF16),
        "kv_gain": row(mla_kv_gain[l]),
        "w_uk": uk.reshape(MLA_KV_RANK, GROUP_HEADS * LANES).astype(BF16),
        "w_uv": uv.reshape(MLA_KV_RANK, GROUP_WIDTH).astype(BF16),
        "cq_gain": (ax_q_gain[l] * (scale * LOG2E))[:, None].astype(F32),
        "ck_gain": ax_k_gain[l][:, None].astype(F32),
        "na_bias": _na_bias_table(na_rpb[l]),
        "sink": sw_sink[l].astype(F32),
        "w_out": w_out[l].astype(BF16),
        "post_gain": row(mix_post_gain[l]),
        "ffn_pre_gain": row(ffn_pre_gain[l]),
        "ffn_post_gain": row(ffn_post_gain[l]),
    }


def kernel(x, mix_pre_gain, w_in, na_rpb, mla_q_gain, mla_w_uq, mla_kv_gain, mla_w_ukv, ax_q_gain,
           ax_k_gain, sw_sink, w_out, mix_post_gain, ffn_pre_gain, w_up, conv_w, conv_b, w_down,
           ffn_post_gain):
    b, s, d = x.shape
    assert d == D_MODEL and s % max(TOKEN_TILE, FLASH_TQ, NA_ROWS_PER_STEP * GRID_W, SW_TQ) == 0
    assert s // GRID_W >= NA_WIN_ROWS
    params = (mix_pre_gain, w_in, na_rpb, mla_q_gain, mla_w_uq, mla_kv_gain, mla_w_ukv, ax_q_gain,
              ax_k_gain, sw_sink, w_out, mix_post_gain, ffn_pre_gain, w_up, conv_w, conv_b, w_down,
              ffn_post_gain)
    tabs = _rope_tables(s)
    mlp = {"w_up": w_up.astype(BF16), "w_down": w_down.astype(BF16),
           "conv_w": conv_w.astype(F32), "conv_b": conv_b.astype(F32)[:, None, :]}
    for l in range(w_in.shape[0]):
        lw = _layer_weights(l, *params)
        (a_q, a_k, a_v, b_qT, b_k, b_vT, c_qT, c_k, c_vT, d_q, d_k, d_v) = _projection(
            x, lw, tabs, TOKEN_TILE)
        o_a = _neighbourhood(a_q, a_k, a_v, lw["na_bias"])
        o_bT = _flash(b_qT, b_k, b_vT, n_heads=GROUP_HEADS, n_kv=GROUP_HEADS, dk=LANES,
                      k_head_major=False, tq=FLASH_TQ * FLASH_STREAMS, n_streams=FLASH_STREAMS)
        o_cT = _flash(c_qT, c_k, c_vT, n_heads=GROUP_HEADS, n_kv=GROUP_KV_HEADS, dk=HEAD_DIM,
                      k_head_major=True, tq=FLASH_TQ * FLASH_STREAMS, n_streams=FLASH_STREAMS)
        o_d = _sliding_window(d_q, d_k, d_v, lw["sink"], SW_TQ)
        x = _out_projection(x, o_a, o_bT, o_cT, o_d, lw["w_out"], lw["post_gain"], TOKEN_TILE)
        x = _ffn(x, lw, mlp, l, FFN_TILE)
    return x
```

```python
import functools
import math

import numpy as np
import jax
import jax.numpy as jnp
from jax import lax
from jax.experimental import pallas as pl
from jax.experimental.pallas import tpu as pltpu

F32 = jnp.float32
BF16 = jnp.bfloat16

D_MODEL = 1024
GRID_W = 64
HEAD_DIM = 64
GROUP_HEADS = 4
GROUP_KV_HEADS = 2
GROUP_WIDTH = GROUP_HEADS * HEAD_DIM
ROPE_THETA = 10000.0
NORM_EPS = 1e-6
MASK_VALUE = -1e30
NA_WIN_ROWS = 8
NA_WIN_COLS = 16
MLA_Q_RANK = 256
MLA_KV_RANK = 128
MLA_NOPE = 64
MLA_ROPE = 32
SW_WINDOW = 128
D_FF = 2816
LOG2E = math.log2(math.e)

LANES = 128
VMEM_LIMIT = 56 * 1024 * 1024

TOKEN_TILE = 1024
FLASH_TQ = 512
FLASH_TK = 256
FLASH_STREAMS = 2
FLASH_PAIRS_PER_TRIP = 5
V_EXT = HEAD_DIM + 16
NA_ROWS_PER_STEP = 8
NA_ROWS_INTERLEAVED = 4
SW_TQ = 256
FFN_TILE = 1024
FFN_PAIRS_PER_TRIP = 1
FF_CHUNK = 256


def _rms(x, gain):
    return x * lax.rsqrt(jnp.mean(x * x, axis=-1, keepdims=True) + NORM_EPS) * gain


def _rope_lanes(x, tab_ref, half):
    w = x.shape[-1]
    return (x * tab_ref[0] + pltpu.roll(x, w - half, 1) * tab_ref[1]
            + pltpu.roll(x, half, 1) * tab_ref[2])


def _bdot(a, b):
    return jnp.dot(a, b, preferred_element_type=F32)


_C_AQ, _C_AK, _C_AV = 0, 256, 512
_C_BCQ, _C_BCKV, _C_BKR = 768, 1024, 1152
_C_CQ, _C_CK, _C_CV = 1280, 1536, 1664
_C_DQ, _C_DK, _C_DV = 1792, 2048, 2176
_IN_COLS_PADDED = 2304
_PAIR = 2 * LANES


def _proj_kernel(x_ref, g_ref, win_ref, qg_ref, wuq_ref, kvg_ref, wuk_ref, wuv_ref,
                 cqg_ref, ckg_ref, tabb_ref, tabd_ref, tabc_ref,
                 aq_ref, ak_ref, av_ref, bqT_ref, bk_ref, bvT_ref,
                 cqT_ref, ck_ref, cvT_ref, dq_ref, dk_ref, dv_ref):
    h = _rms(x_ref[0], g_ref[...]).astype(BF16)

    def proj(c0, width):
        return _bdot(h, win_ref[:, c0:c0 + width])

    def store_heads(ref, z, n_heads):
        for hd in range(n_heads):
            ref[0, hd] = z[:, hd * HEAD_DIM:(hd + 1) * HEAD_DIM].astype(BF16)

    def store_vT_ext(ref, vT, n_heads):
        tk = ref.shape[3]
        pad = V_EXT - HEAD_DIM
        ones_row = (lax.broadcasted_iota(jnp.int32, (pad, tk), 0) == 0).astype(F32).astype(BF16)
        for t in range(ref.shape[1]):
            for hd in range(n_heads):
                ref[0, t, hd * V_EXT:hd * V_EXT + HEAD_DIM, :] = (
                    vT[hd * HEAD_DIM:(hd + 1) * HEAD_DIM, t * tk:(t + 1) * tk].astype(BF16))
                ref[0, t, hd * V_EXT + HEAD_DIM:(hd + 1) * V_EXT, :] = ones_row

    store_heads(aq_ref, proj(_C_AQ, GROUP_WIDTH), GROUP_HEADS)
    store_heads(ak_ref, proj(_C_AK, GROUP_WIDTH), GROUP_HEADS)
    store_heads(av_ref, proj(_C_AV, GROUP_WIDTH), GROUP_HEADS)

    cq = _rms(proj(_C_BCQ, MLA_Q_RANK), qg_ref[...]).astype(BF16)
    qb = _bdot(cq, wuq_ref[...]) * ((MLA_NOPE + MLA_ROPE) ** -0.5 * LOG2E)
    nope_w = GROUP_HEADS * MLA_NOPE
    q_nope_T = qb[:, :nope_w].T
    q_pe_T = qb[:, nope_w:].T
    half = MLA_ROPE // 2
    cos_m = tabc_ref[HEAD_DIM:HEAD_DIM + half]
    sin_m = tabc_ref[HEAD_DIM + half:HEAD_DIM + 2 * half]
    for hd in range(GROUP_HEADS):
        r0 = hd * LANES
        x1 = q_pe_T[hd * MLA_ROPE:hd * MLA_ROPE + half]
        x2 = q_pe_T[hd * MLA_ROPE + half:(hd + 1) * MLA_ROPE]
        bqT_ref[0, r0:r0 + MLA_NOPE, :] = q_nope_T[hd * MLA_NOPE:(hd + 1) * MLA_NOPE].astype(BF16)
        bqT_ref[0, r0 + MLA_NOPE:r0 + MLA_NOPE + half, :] = (x1 * cos_m - x2 * sin_m).astype(BF16)
        bqT_ref[0, r0 + MLA_NOPE + half:r0 + MLA_NOPE + MLA_ROPE, :] = (
            x2 * cos_m + x1 * sin_m).astype(BF16)
        bqT_ref[0, r0 + MLA_NOPE + MLA_ROPE:r0 + LANES, :] = jnp.zeros(
            (LANES - MLA_NOPE - MLA_ROPE, qb.shape[0]), BF16)
    assert (_C_BKR, _C_CV, _C_DV) == (_C_BCKV + LANES, _C_CK + LANES, _C_DK + LANES)
    b_pair = proj(_C_BCKV, _PAIR)
    kpe = _rope_lanes(b_pair[:, LANES:], tabb_ref, MLA_ROPE // 2)
    ckv = _rms(b_pair[:, :LANES], kvg_ref[...]).astype(BF16)
    kn = _bdot(ckv, wuk_ref[...])
    for hd in range(GROUP_HEADS):
        blk = slice(hd * LANES, (hd + 1) * LANES)
        bk_ref[0, :, blk] = (kn[:, blk] + kpe).astype(BF16)
    store_vT_ext(bvT_ref, _bdot(ckv, wuv_ref[...]).T, GROUP_HEADS)

    def norm_rope_T(blk, gain_col):
        ms = jnp.mean(blk * blk, axis=0, keepdims=True)
        blk = blk * lax.rsqrt(ms + NORM_EPS) * gain_col
        q = HEAD_DIM // 4
        cr, sr = tabc_ref[0:q], tabc_ref[q:2 * q]
        cc, sc = tabc_ref[2 * q:3 * q], tabc_ref[3 * q:4 * q]
        x1, x2, x3, x4 = blk[0:q], blk[q:2 * q], blk[2 * q:3 * q], blk[3 * q:4 * q]
        return jnp.concatenate([x1 * cr - x2 * sr, x2 * cr + x1 * sr,
                                x3 * cc - x4 * sc, x4 * cc + x3 * sc], axis=0)

    cqT = proj(_C_CQ, GROUP_WIDTH).T
    for hd in range(GROUP_HEADS):
        rows = slice(hd * HEAD_DIM, (hd + 1) * HEAD_DIM)
        cqT_ref[0, rows, :] = norm_rope_T(cqT[rows], cqg_ref[...]).astype(BF16)
    c_pair = proj(_C_CK, _PAIR)
    ckT = c_pair[:, :LANES].T
    ck = jnp.concatenate(
        [norm_rope_T(ckT[hd * HEAD_DIM:(hd + 1) * HEAD_DIM], ckg_ref[...])
         for hd in range(GROUP_KV_HEADS)], axis=0).T
    store_heads(ck_ref, ck, GROUP_KV_HEADS)
    store_vT_ext(cvT_ref, c_pair[:, LANES:].T, GROUP_KV_HEADS)

    dq = proj(_C_DQ, GROUP_WIDTH)
    dq = jnp.concatenate([_rope_lanes(dq[:, j * LANES:(j + 1) * LANES], tabd_ref, HEAD_DIM // 2)
                          for j in range(GROUP_WIDTH // LANES)], axis=1)
    store_heads(dq_ref, dq, GROUP_HEADS)
    d_pair = proj(_C_DK, _PAIR)
    dk = _rope_lanes(d_pair[:, :LANES], tabd_ref, HEAD_DIM // 2)
    store_heads(dk_ref, dk, GROUP_KV_HEADS)
    store_heads(dv_ref, d_pair[:, LANES:], GROUP_KV_HEADS)


def _const_spec(shape):
    n = len(shape)
    return pl.BlockSpec(shape, lambda *_: (0,) * n, pipeline_mode=pl.Buffered(1))


def _projection(x, lw, tabs, tm):
    b, s, _ = x.shape
    nt = s // tm
    nk = s // FLASH_TK
    assert tm % FLASH_TK == 0
    head_q = jax.ShapeDtypeStruct((b, GROUP_HEADS, s, HEAD_DIM), BF16)
    head_kv = jax.ShapeDtypeStruct((b, GROUP_KV_HEADS, s, HEAD_DIM), BF16)
    out_shape = (
        head_q, head_q, head_q,
        jax.ShapeDtypeStruct((b, GROUP_HEADS * LANES, s), BF16),
        jax.ShapeDtypeStruct((b, s, GROUP_HEADS * LANES), BF16),
        jax.ShapeDtypeStruct((b, nk, GROUP_HEADS * V_EXT, FLASH_TK), BF16),
        jax.ShapeDtypeStruct((b, GROUP_WIDTH, s), BF16),
        head_kv,
        jax.ShapeDtypeStruct((b, nk, GROUP_KV_HEADS * V_EXT, FLASH_TK), BF16),
        head_q, head_kv, head_kv,
    )
    hq_spec = pl.BlockSpec((1, GROUP_HEADS, tm, HEAD_DIM), lambda bi, i: (bi, 0, i, 0))
    hkv_spec = pl.BlockSpec((1, GROUP_KV_HEADS, tm, HEAD_DIM), lambda bi, i: (bi, 0, i, 0))
    out_specs = (
        hq_spec, hq_spec, hq_spec,
        pl.BlockSpec((1, GROUP_HEADS * LANES, tm), lambda bi, i: (bi, 0, i)),
        pl.BlockSpec((1, tm, GROUP_HEADS * LANES), lambda bi, i: (bi, i, 0)),
        pl.BlockSpec((1, tm // FLASH_TK, GROUP_HEADS * V_EXT, FLASH_TK),
                     lambda bi, i: (bi, i, 0, 0)),
        pl.BlockSpec((1, GROUP_WIDTH, tm), lambda bi, i: (bi, 0, i)),
        hkv_spec,
        pl.BlockSpec((1, tm // FLASH_TK, GROUP_KV_HEADS * V_EXT, FLASH_TK),
                     lambda bi, i: (bi, i, 0, 0)),
        hq_spec, hkv_spec, hkv_spec,
    )
    in_specs = [
        pl.BlockSpec((1, tm, D_MODEL), lambda bi, i: (bi, i, 0)),
        _const_spec((1, D_MODEL)),
        _const_spec((D_MODEL, _IN_COLS_PADDED)),
        _const_spec((1, MLA_Q_RANK)),
        _const_spec((MLA_Q_RANK, GROUP_HEADS * (MLA_NOPE + MLA_ROPE))),
        _const_spec((1, MLA_KV_RANK)),
        _const_spec((MLA_KV_RANK, GROUP_HEADS * LANES)),
        _const_spec((MLA_KV_RANK, GROUP_WIDTH)),
        _const_spec((HEAD_DIM, 1)),
        _const_spec((HEAD_DIM, 1)),
        pl.BlockSpec((3, tm, LANES), lambda bi, i: (0, i, 0)),
        pl.BlockSpec((3, tm, LANES), lambda bi, i: (0, i, 0)),
        pl.BlockSpec((HEAD_DIM + MLA_ROPE, tm), lambda bi, i: (0, i)),
    ]
    return pl.pallas_call(
        _proj_kernel,
        grid=(b, nt),
        in_specs=in_specs,
        out_specs=out_specs,
        out_shape=out_shape,
        compiler_params=pltpu.CompilerParams(
            dimension_semantics=("parallel", "parallel"), vmem_limit_bytes=VMEM_LIMIT),
        name="projection",
    )(x, lw["pre_gain"], lw["w_in"], lw["q_gain"], lw["w_uq"], lw["kv_gain"], lw["w_uk"],
      lw["w_uv"], lw["cq_gain"], lw["ck_gain"], tabs["mla"], tabs["full"], tabs["axial"])


def _flash_kernel(qT_ref, k_ref, vT_ref, oT_ref, s_even, s_odd, *, tk, n_chunks, n_streams):
    tq = qT_ref.shape[2] // n_streams
    dv_ext = vT_ref.shape[2]

    def produce(s_ref, c):
        k = k_ref[0, pl.ds(pl.multiple_of(c * tk, tk), tk), :]
        maxes = []
        for st in range(n_streams):
            s = _bdot(k, qT_ref[0, :, st * tq:(st + 1) * tq])
            s_ref[st] = s
            maxes.append(jnp.max(s, axis=0, keepdims=True))
        return tuple(maxes)

    def consume(s_ref, chunk_max, c, carry):
        vT = vT_ref[0, c]
        out = []
        for st, (m, acc) in enumerate(carry):
            m_new = jnp.maximum(m, chunk_max[st])
            p = jnp.exp2(s_ref[st] - m_new).astype(BF16)
            acc = jnp.exp2(m - m_new) * acc + _bdot(vT, p)
            out.append((m_new, acc))
        return tuple(out)

    def pair(jj, state):
        carry, max_even = state
        max_odd = produce(s_odd, 2 * jj + 1)
        carry = consume(s_even, max_even, 2 * jj, carry)
        max_even = produce(s_even, 2 * jj + 2)
        return consume(s_odd, max_odd, 2 * jj + 1, carry), max_even

    def body(t, state):
        for u in range(FLASH_PAIRS_PER_TRIP):
            state = pair(t * FLASH_PAIRS_PER_TRIP + u, state)
        return state

    carry = tuple((jnp.full((1, tq), -jnp.inf, F32), jnp.zeros((dv_ext, tq), F32))
                  for _ in range(n_streams))
    max_even = produce(s_even, 0)
    n_pairs = n_chunks // 2 - 1
    n_trips = n_pairs // FLASH_PAIRS_PER_TRIP
    state = lax.fori_loop(0, n_trips, body, (carry, max_even))
    for jj in range(n_trips * FLASH_PAIRS_PER_TRIP, n_pairs):
        state = pair(jj, state)
    carry, max_even = state
    max_odd = produce(s_odd, n_chunks - 1)
    carry = consume(s_even, max_even, n_chunks - 2, carry)
    carry = consume(s_odd, max_odd, n_chunks - 1, carry)
    for st, (_, acc) in enumerate(carry):
        oT_ref[0, :, st * tq:(st + 1) * tq] = acc[:HEAD_DIM] / acc[HEAD_DIM:HEAD_DIM + 1]


def _flash(qT, k, vT, *, n_heads, n_kv, dk, k_head_major, tq, n_streams):
    b, _, s = qT.shape
    n_chunks, tk = vT.shape[1], vT.shape[3]
    assert n_chunks % 2 == 0 and s % tq == 0
    dv = HEAD_DIM
    rep = n_heads // n_kv
    if k_head_major:
        k_spec = pl.BlockSpec((None, 1, s, dk), lambda bi, h, i: (bi, h // rep, 0, 0))
    else:
        k_spec = pl.BlockSpec((1, s, dk), lambda bi, h, i: (bi, 0, h // rep))
    return pl.pallas_call(
        functools.partial(_flash_kernel, tk=tk, n_chunks=n_chunks, n_streams=n_streams),
        grid=(b, n_heads, s // tq),
        in_specs=[
            pl.BlockSpec((1, dk, tq), lambda bi, h, i: (bi, h, i)),
            k_spec,
            pl.BlockSpec((1, n_chunks, V_EXT, tk), lambda bi, h, i: (bi, 0, h // rep, 0)),
        ],
        out_specs=pl.BlockSpec((1, dv, tq), lambda bi, h, i: (bi, h, i)),
        out_shape=jax.ShapeDtypeStruct((b, n_heads * dv, s), F32),
        scratch_shapes=[pltpu.VMEM((n_streams, tk, tq // n_streams), F32)] * 2,
        compiler_params=pltpu.CompilerParams(
            dimension_semantics=("parallel", "parallel", "parallel"),
            vmem_limit_bytes=VMEM_LIMIT),
        name="dense_attention",
    )(qT, k, vT)


def _na_kernel(q_ref, kp_ref, kc_ref, kn_ref, vp_ref, vc_ref, vn_ref, bias_ref, o_ref,
               k_win, v_win, *, n_rows):
    i = pl.program_id(1)
    blk = NA_ROWS_PER_STEP * GRID_W
    band = NA_WIN_ROWS * GRID_W
    for w, (kr, vr) in enumerate(((kp_ref, vp_ref), (kc_ref, vc_ref), (kn_ref, vn_ref))):
        k_win[:, w * blk:(w + 1) * blk, :] = kr[0]
        v_win[:, w * blk:(w + 1) * blk, :] = vr[0]
    heads = range(GROUP_HEADS)
    for j0 in range(0, NA_ROWS_PER_STEP, NA_ROWS_INTERLEAVED):
        rows = range(j0, j0 + NA_ROWS_INTERLEAVED)
        offs, ds = {}, {}
        for j in rows:
            r = i * NA_ROWS_PER_STEP + j
            rs = jnp.clip(r - NA_WIN_ROWS // 2, 0, n_rows - NA_WIN_ROWS)
            offs[j] = pl.multiple_of((rs - (i - 1) * NA_ROWS_PER_STEP) * GRID_W, GRID_W)
            ds[j] = r - rs
        ss = {(j, hd): lax.dot_general(q_ref[0, hd, j * GRID_W:(j + 1) * GRID_W, :],
                                       k_win[hd, pl.ds(offs[j], band), :],
                                       (((1,), (1,)), ((), ())), preferred_element_type=F32)
              + bias_ref[hd, ds[j]] for j in rows for hd in heads}
        ps = {key: jnp.exp(s - jnp.max(s, axis=-1, keepdims=True)) for key, s in ss.items()}
        for j in rows:
            os = [_bdot(ps[j, hd].astype(BF16), v_win[hd, pl.ds(offs[j], band), :])
                  / jnp.sum(ps[j, hd], axis=-1, keepdims=True) for hd in heads]
            o_ref[0, j * GRID_W:(j + 1) * GRID_W, :] = jnp.concatenate(os, axis=1)


def _neighbourhood(q, k, v, bias):
    b, nh, s, hd = q.shape
    blk = NA_ROWS_PER_STEP * GRID_W
    nb = s // blk
    n_rows = s // GRID_W
    cur = lambda bi, i: (bi, 0, i, 0)
    prev = lambda bi, i: (bi, 0, jnp.maximum(i - 1, 0), 0)
    nxt = lambda bi, i: (bi, 0, jnp.minimum(i + 1, nb - 1), 0)
    spec = lambda f: pl.BlockSpec((1, nh, blk, hd), f)
    return pl.pallas_call(
        functools.partial(_na_kernel, n_rows=n_rows),
        grid=(b, nb),
        in_specs=[spec(cur), spec(prev), spec(cur), spec(nxt), spec(prev), spec(cur), spec(nxt),
                  _const_spec(bias.shape)],
        out_specs=pl.BlockSpec((1, blk, nh * hd), lambda bi, i: (bi, i, 0)),
        out_shape=jax.ShapeDtypeStruct((b, s, nh * hd), F32),
        scratch_shapes=[pltpu.VMEM((nh, 3 * blk, hd), BF16), pltpu.VMEM((nh, 3 * blk, hd), BF16)],
        compiler_params=pltpu.CompilerParams(
            dimension_semantics=("parallel", "parallel"), vmem_limit_bytes=VMEM_LIMIT),
        name="neighbourhood_attention",
    )(q, k, k, k, v, v, v, bias)


def _na_bias_table(rpb):
    c = np.arange(GRID_W)[:, None]
    kc = np.arange(GRID_W)[None, :]
    cs = np.clip(c - NA_WIN_COLS // 2, 0, GRID_W - NA_WIN_COLS)
    valid = (kc >= cs) & (kc < cs + NA_WIN_COLS)
    col_off = kc - c + (NA_WIN_COLS - 1)
    n_off = 2 * NA_WIN_COLS - 1
    select = (valid[:, :, None] & (col_off[:, :, None] == np.arange(n_off))).astype(np.float32)
    x = jnp.einsum("hro,cko->hrck", rpb.astype(F32), jnp.asarray(select),
                   precision=lax.Precision.HIGHEST)
    x = jnp.where(valid[None, None], x, MASK_VALUE)
    t = jnp.stack([x[:, NA_WIN_ROWS - 1 - d:2 * NA_WIN_ROWS - 1 - d] for d in range(NA_WIN_ROWS)],
                  axis=1)
    t = t.transpose(0, 1, 3, 2, 4)
    return t.reshape(rpb.shape[0], NA_WIN_ROWS, GRID_W, NA_WIN_ROWS * GRID_W)


def _sw_kernel(sink_ref, q_ref, kp_ref, kc_ref, kn_ref, vp_ref, vc_ref, vn_ref, o_ref, *, seq):
    i = pl.program_id(1)
    tq = q_ref.shape[2]
    span = tq + 2 * SW_WINDOW
    t0 = i * tq
    row = lax.broadcasted_iota(jnp.int32, (tq, span), 0)
    col = lax.broadcasted_iota(jnp.int32, (tq, span), 1)
    kpos = col + (t0 - SW_WINDOW)
    rel = col - row
    valid = (rel >= 0) & (rel <= 2 * SW_WINDOW) & (kpos >= 0) & (kpos < seq)
    nt = (((1,), (1,)), ((), ()))
    rep = GROUP_HEADS // GROUP_KV_HEADS
    heads = range(GROUP_HEADS)
    ss = [jnp.where(valid, jnp.concatenate(
        [lax.dot_general(q_ref[0, hd], kr[0, hd // rep], nt, preferred_element_type=F32)
         for kr in (kp_ref, kc_ref, kn_ref)], axis=1), MASK_VALUE) for hd in heads]
    ms = [jnp.maximum(jnp.max(ss[hd], axis=-1, keepdims=True), sink_ref[hd]) for hd in heads]
    ps = [jnp.exp(ss[hd] - ms[hd]) for hd in heads]
    denoms = [jnp.sum(ps[hd], axis=-1, keepdims=True) + jnp.exp(sink_ref[hd] - ms[hd])
              for hd in heads]
    os = []
    for hd in heads:
        g = hd // rep
        pb = ps[hd].astype(BF16)
        o = (_bdot(pb[:, :SW_WINDOW], vp_ref[0, g])
             + _bdot(pb[:, SW_WINDOW:SW_WINDOW + tq], vc_ref[0, g])
             + _bdot(pb[:, SW_WINDOW + tq:], vn_ref[0, g]))
        os.append(o / denoms[hd])
    o_ref[0] = jnp.concatenate(os, axis=1)


def _sliding_window(q, k, v, sink, tq):
    b, nh, s, hd = q.shape
    nkv = k.shape[1]
    nb = s // tq
    r = tq // SW_WINDOW
    n_small = s // SW_WINDOW
    cur = pl.BlockSpec((1, nkv, tq, hd), lambda bi, i: (bi, 0, i, 0))
    prev = pl.BlockSpec((1, nkv, SW_WINDOW, hd), lambda bi, i: (bi, 0, jnp.maximum(i * r - 1, 0), 0))
    nxt = pl.BlockSpec((1, nkv, SW_WINDOW, hd),
                       lambda bi, i: (bi, 0, jnp.minimum((i + 1) * r, n_small - 1), 0))
    return pl.pallas_call(
        functools.partial(_sw_kernel, seq=s),
        grid=(b, nb),
        in_specs=[pl.BlockSpec(memory_space=pltpu.SMEM),
                  pl.BlockSpec((1, nh, tq, hd), lambda bi, i: (bi, 0, i, 0)),
                  prev, cur, nxt, prev, cur, nxt],
        out_specs=pl.BlockSpec((1, tq, nh * hd), lambda bi, i: (bi, i, 0)),
        out_shape=jax.ShapeDtypeStruct((b, s, nh * hd), F32),
        compiler_params=pltpu.CompilerParams(
            dimension_semantics=("parallel", "parallel"), vmem_limit_bytes=VMEM_LIMIT),
        name="sliding_window_attention",
    )(sink, q, k, k, k, v, v, v)


def _out_kernel(x_ref, oa_ref, obT_ref, ocT_ref, od_ref, w_ref, g_ref, o_ref):
    mixed_in = jnp.concatenate(
        [oa_ref[0].astype(BF16), obT_ref[0].T.astype(BF16), ocT_ref[0].T.astype(BF16),
         od_ref[0].astype(BF16)], axis=1)
    mixed = _bdot(mixed_in, w_ref[...])
    o_ref[0] = x_ref[0] + _rms(mixed, g_ref[...])


def _out_projection(x, o_a, o_bT, o_cT, o_d, w_out, gain, tm):
    b, s, _ = x.shape
    tok = lambda w: pl.BlockSpec((1, tm, w), lambda bi, i: (bi, i, 0))
    feat = pl.BlockSpec((1, GROUP_WIDTH, tm), lambda bi, i: (bi, 0, i))
    return pl.pallas_call(
        _out_kernel,
        grid=(b, s // tm),
        in_specs=[tok(D_MODEL), tok(GROUP_WIDTH), feat, feat, tok(GROUP_WIDTH),
                  _const_spec((D_MODEL, D_MODEL)), _const_spec((1, D_MODEL))],
        out_specs=tok(D_MODEL),
        out_shape=jax.ShapeDtypeStruct(x.shape, F32),
        compiler_params=pltpu.CompilerParams(
            dimension_semantics=("parallel", "parallel"), vmem_limit_bytes=VMEM_LIMIT),
        name="out_projection",
    )(x, o_a, o_bT, o_cT, o_d, w_out, gain)


FFN_HALO = 8


def _ffn_kernel(x_ref, xp_ref, xn_ref, g_ref, wup_ref, cw_ref, cb_ref, wd_ref, pg_ref, o_ref,
                h_scr, acc_scr, u_even, u_odd, *, fc):
    n_chunks = D_FF // fc

    def cols(ref, c, base):
        return ref[:, pl.ds(pl.multiple_of(base + c * fc, LANES), fc)]

    i = pl.program_id(1)
    n_tiles = pl.num_programs(1)
    tm = x_ref.shape[1]
    ext = tm + 2 * FFN_HALO
    g = g_ref[...]
    hp = _rms(xp_ref[0], g) * (i > 0).astype(F32)
    hn = _rms(xn_ref[0], g) * (i < n_tiles - 1).astype(F32)
    h_scr[...] = jnp.concatenate([hp, _rms(x_ref[0], g), hn], axis=0).astype(BF16)
    acc_scr[...] = jnp.zeros_like(acc_scr)

    def produce(u_ref, c):
        hh = h_scr[...]
        u_ref[0] = _bdot(hh, cols(wup_ref, c, 0))
        u_ref[1] = _bdot(hh, cols(wup_ref, c, D_FF))

    def conv(u_ref, cw, cb):
        lo = FFN_HALO - 1
        return (cb + u_ref[lo:lo + tm] * cw[0:1] + u_ref[lo + 1:lo + 1 + tm] * cw[1:2]
                + u_ref[lo + 2:lo + 2 + tm] * cw[2:3])

    def consume(u_ref, c):
        gate = conv(u_ref.at[0], cols(cw_ref, c, 0), cols(cb_ref, c, 0))
        val = conv(u_ref.at[1], cols(cw_ref, c, D_FF), cols(cb_ref, c, D_FF))
        act = jax.nn.gelu(gate, approximate=True) * val
        acc_scr[...] += _bdot(act.astype(BF16), wd_ref[pl.ds(pl.multiple_of(c * fc, fc), fc), :])

    def pair(jj):
        produce(u_odd, 2 * jj + 1)
        consume(u_even, 2 * jj)
        produce(u_even, 2 * jj + 2)
        consume(u_odd, 2 * jj + 1)

    def body(t, carry):
        for u in range(FFN_PAIRS_PER_TRIP):
            pair(t * FFN_PAIRS_PER_TRIP + u)
        return carry

    n_pairs = (n_chunks - 1) // 2
    tail = n_chunks - 2 * n_pairs
    n_trips = n_pairs // FFN_PAIRS_PER_TRIP
    produce(u_even, 0)
    lax.fori_loop(0, n_trips, body, 0)
    for jj in range(n_trips * FFN_PAIRS_PER_TRIP, n_pairs):
        pair(jj)
    if tail == 2:
        produce(u_odd, n_chunks - 1)
    consume(u_even, 2 * n_pairs)
    if tail == 2:
        consume(u_odd, n_chunks - 1)
    o_ref[0] = x_ref[0] + _rms(acc_scr[...], pg_ref[...])


def _layer_spec(stacked, l):
    shape = stacked.shape[1:]
    return pl.BlockSpec((None,) + shape, lambda *_: (l,) + (0,) * len(shape),
                        pipeline_mode=pl.Buffered(1))


def _ffn(x, lw, mlp, l, tm):
    b, s, _ = x.shape
    fc = FF_CHUNK
    assert D_FF % fc == 0 and fc % LANES == 0
    r = tm // FFN_HALO
    n_halo = s // FFN_HALO
    tile = pl.BlockSpec((1, tm, D_MODEL), lambda bi, i: (bi, i, 0))
    prev = pl.BlockSpec((1, FFN_HALO, D_MODEL), lambda bi, i: (bi, jnp.maximum(i * r - 1, 0), 0))
    nxt = pl.BlockSpec((1, FFN_HALO, D_MODEL),
                       lambda bi, i: (bi, jnp.minimum((i + 1) * r, n_halo - 1), 0))
    return pl.pallas_call(
        functools.partial(_ffn_kernel, fc=fc),
        grid=(b, s // tm),
        in_specs=[tile, prev, nxt, _const_spec((1, D_MODEL)),
                  _layer_spec(mlp["w_up"], l), _layer_spec(mlp["conv_w"], l),
                  _layer_spec(mlp["conv_b"], l), _layer_spec(mlp["w_down"], l),
                  _const_spec((1, D_MODEL))],
        out_specs=tile,
        out_shape=jax.ShapeDtypeStruct(x.shape, F32),
        scratch_shapes=[pltpu.VMEM((tm + 2 * FFN_HALO, D_MODEL), BF16),
                        pltpu.VMEM((tm, D_MODEL), F32),
                        pltpu.VMEM((2, tm + 2 * FFN_HALO, fc), F32),
                        pltpu.VMEM((2, tm + 2 * FFN_HALO, fc), F32)],
        compiler_params=pltpu.CompilerParams(
            dimension_semantics=("parallel", "parallel"), vmem_limit_bytes=VMEM_LIMIT),
        name="conv_mlp",
    )(x, x, x, lw["ffn_pre_gain"], mlp["w_up"], mlp["conv_w"], mlp["conv_b"], mlp["w_down"],
      lw["ffn_post_gain"])


def _rope_tables(s):
    t = jnp.arange(s)

    def angles(pos, dim):
        inv = ROPE_THETA ** (-jnp.arange(0, dim, 2, dtype=F32) / dim)
        return pos.astype(F32)[:, None] * inv[None, :]

    def lane_table(ang, lead, trail, reps):
        half = ang.shape[1]
        cos, sin, zero = jnp.cos(ang), jnp.sin(ang), jnp.zeros_like(ang)
        one = lambda n: jnp.ones((s, n), F32)
        nul = lambda n: jnp.zeros((s, n), F32)
        c = jnp.concatenate([one(lead)] + [cos, cos] * reps + [one(trail)], axis=1)
        lo = jnp.concatenate([nul(lead)] + [-sin, zero] * reps + [nul(trail)], axis=1)
        hi = jnp.concatenate([nul(lead)] + [zero, sin] * reps + [nul(trail)], axis=1)
        assert c.shape[1] == LANES and 2 * half * reps + lead + trail == LANES
        return jnp.stack([c, lo, hi])

    ang_row = angles(t // GRID_W, HEAD_DIM // 2)
    ang_col = angles(t % GRID_W, HEAD_DIM // 2)
    ang_mla = angles(t, MLA_ROPE)
    return {
        "mla": lane_table(ang_mla, MLA_NOPE, LANES - MLA_NOPE - MLA_ROPE, 1),
        "full": lane_table(angles(t, HEAD_DIM), 0, 0, LANES // HEAD_DIM),
        "axial": jnp.concatenate([jnp.cos(ang_row), jnp.sin(ang_row),
                                  jnp.cos(ang_col), jnp.sin(ang_col),
                                  jnp.cos(ang_mla), jnp.sin(ang_mla)], axis=1).T,
    }


def _layer_weights(l, mix_pre_gain, w_in, na_rpb, mla_q_gain, mla_w_uq, mla_kv_gain, mla_w_ukv,
                   ax_q_gain, ax_k_gain, sw_sink, w_out, mix_post_gain, ffn_pre_gain, w_up,
                   conv_w, conv_b, w_down, ffn_post_gain):
    gw, kvw = GROUP_WIDTH, GROUP_KV_HEADS * HEAD_DIM
    sizes = (gw, gw, gw, MLA_Q_RANK, MLA_KV_RANK, MLA_ROPE, gw, kvw, kvw, gw, kvw, kvw)
    bounds = np.cumsum((0,) + sizes)
    (a_q, a_k, a_v, b_cq, b_ckv, b_kr, c_q, c_k, c_v, d_q, d_k, d_v) = [
        w_in[l][:, bounds[j]:bounds[j + 1]] for j in range(len(sizes))]
    scale = HEAD_DIM ** -0.5
    zeros = lambda n: jnp.zeros((D_MODEL, n), F32)
    kr_block = jnp.concatenate([zeros(MLA_NOPE), b_kr, zeros(LANES - MLA_NOPE - MLA_ROPE)], axis=1)
    w_in_r = jnp.concatenate([a_q * scale, a_k, a_v, b_cq, b_ckv, kr_block, c_q, c_k, c_v,
                              d_q * scale, d_k, d_v], axis=1)
    assert w_in_r.shape[1] == _IN_COLS_PADDED

    uq = mla_w_uq[l].reshape(MLA_Q_RANK, GROUP_HEADS, MLA_NOPE + MLA_ROPE)
    uq = jnp.concatenate([uq[:, :, :MLA_NOPE].reshape(MLA_Q_RANK, -1),
                          uq[:, :, MLA_NOPE:].reshape(MLA_Q_RANK, -1)], axis=1)
    ukv = mla_w_ukv[l].reshape(MLA_KV_RANK, GROUP_HEADS, MLA_NOPE + HEAD_DIM)
    uk = jnp.pad(ukv[:, :, :MLA_NOPE], ((0, 0), (0, 0), (0, LANES - MLA_NOPE)))
    uv = ukv[:, :, MLA_NOPE:]

    row = lambda v: v[None, :].astype(F32)
    return {
        "pre_gain": row(mix_pre_gain[l]),
        "w_in": w_in_r.astype(BF16),
        "q_gain": row(mla_q_gain[l]),
        "w_uq": uq.astype(BF16),
        "kv_gain": row(mla_kv_gain[l]),
        "w_uk": uk.reshape(MLA_KV_RANK, GROUP_HEADS * LANES).astype(BF16),
        "w_uv": uv.reshape(MLA_KV_RANK, GROUP_WIDTH).astype(BF16),
        "cq_gain": (ax_q_gain[l] * (scale * LOG2E))[:, None].astype(F32),
        "ck_gain": ax_k_gain[l][:, None].astype(F32),
        "na_bias": _na_bias_table(na_rpb[l]),
        "sink": sw_sink[l].astype(F32),
        "w_out": w_out[l].astype(BF16),
        "post_gain": row(mix_post_gain[l]),
        "ffn_pre_gain": row(ffn_pre_gain[l]),
        "ffn_post_gain": row(ffn_post_gain[l]),
    }


def kernel(x, mix_pre_gain, w_in, na_rpb, mla_q_gain, mla_w_uq, mla_kv_gain, mla_w_ukv, ax_q_gain,
           ax_k_gain, sw_sink, w_out, mix_post_gain, ffn_pre_gain, w_up, conv_w, conv_b, w_down,
           ffn_post_gain):
    b, s, d = x.shape
    assert d == D_MODEL and s % max(TOKEN_TILE, FLASH_TQ, NA_ROWS_PER_STEP * GRID_W, SW_TQ) == 0
    assert s // GRID_W >= NA_WIN_ROWS
    params = (mix_pre_gain, w_in, na_rpb, mla_q_gain, mla_w_uq, mla_kv_gain, mla_w_ukv, ax_q_gain,
              ax_k_gain, sw_sink, w_out, mix_post_gain, ffn_pre_gain, w_up, conv_w, conv_b, w_down,
              ffn_post_gain)
    tabs = _rope_tables(s)
    mlp = {"w_up": w_up.astype(BF16), "w_down": w_down.astype(BF16),
           "conv_w": conv_w.astype(F32), "conv_b": conv_b.astype(F32)[:, None, :]}
    for l in range(w_in.shape[0]):
        lw = _layer_weights(l, *params)
        (a_q, a_k, a_v, b_qT, b_k, b_vT, c_qT, c_k, c_vT, d_q, d_k, d_v) = _projection(
            x, lw, tabs, TOKEN_TILE)
        o_a = _neighbourhood(a_q, a_k, a_v, lw["na_bias"])
        o_bT = _flash(b_qT, b_k, b_vT, n_heads=GROUP_HEADS, n_kv=GROUP_HEADS, dk=LANES,
                      k_head_major=False, tq=FLASH_TQ * FLASH_STREAMS, n_streams=FLASH_STREAMS)
        o_cT = _flash(c_qT, c_k, c_vT, n_heads=GROUP_HEADS, n_kv=GROUP_KV_HEADS, dk=HEAD_DIM,
                      k_head_major=True, tq=FLASH_TQ * FLASH_STREAMS, n_streams=FLASH_STREAMS)
        o_d = _sliding_window(d_q, d_k, d_v, lw["sink"], SW_TQ)
        x = _out_projection(x, o_a, o_bT, o_cT, o_d, lw["w_out"], lw["post_gain"], TOKEN_TILE)
        x = _ffn(x, lw, mlp, l, FFN_TILE)
    return x
```

```python
import functools
import math

import numpy as np
import jax
import jax.numpy as jnp
from jax import lax
from jax.experimental import pallas as pl
from jax.experimental.pallas import tpu as pltpu

F32 = jnp.float32
BF16 = jnp.bfloat16

D_MODEL = 1024
GRID_W = 64
HEAD_DIM = 64
GROUP_HEADS = 4
GROUP_KV_HEADS = 2
GROUP_WIDTH = GROUP_HEADS * HEAD_DIM
ROPE_THETA = 10000.0
NORM_EPS = 1e-6
MASK_VALUE = -1e30
NA_WIN_ROWS = 8
NA_WIN_COLS = 16
MLA_Q_RANK = 256
MLA_KV_RANK = 128
MLA_NOPE = 64
MLA_ROPE = 32
SW_WINDOW = 128
D_FF = 2816
LOG2E = math.log2(math.e)

LANES = 128
VMEM_LIMIT = 56 * 1024 * 1024

TOKEN_TILE = 1024
FLASH_TQ = 512
FLASH_TK = 256
FLASH_STREAMS = 2
FLASH_PAIRS_PER_TRIP = 5
V_EXT = HEAD_DIM + 16
NA_ROWS_PER_STEP = 8
NA_HALO_ROWS = NA_WIN_ROWS // 2
NA_ROWS_INTERLEAVED = 4
SW_TQ = 256
FFN_TILE = 1024
FFN_PAIRS_PER_TRIP = 1
FF_CHUNK = 256


def _rms(x, gain):
    return x * lax.rsqrt(jnp.mean(x * x, axis=-1, keepdims=True) + NORM_EPS) * gain


def _rope_lanes(x, tab_ref, half):
    w = x.shape[-1]
    return (x * tab_ref[0] + pltpu.roll(x, w - half, 1) * tab_ref[1]
            + pltpu.roll(x, half, 1) * tab_ref[2])


def _bdot(a, b):
    return jnp.dot(a, b, preferred_element_type=F32)


_C_AQ, _C_AK, _C_AV = 0, 256, 512
_C_BCQ, _C_BCKV, _C_BKR = 768, 1024, 1152
_C_CQ, _C_CK, _C_CV = 1280, 1536, 1664
_C_DQ, _C_DK, _C_DV = 1792, 2048, 2176
_IN_COLS_PADDED = 2304
_PAIR = 2 * LANES


def _proj_kernel(x_ref, g_ref, win_ref, qg_ref, wuq_ref, kvg_ref, wuk_ref, wuv_ref,
                 cqg_ref, ckg_ref, tabb_ref, tabd_ref, tabc_ref,
                 aq_ref, ak_ref, av_ref, bqT_ref, bk_ref, bvT_ref,
                 cqT_ref, ck_ref, cvT_ref, dq_ref, dk_ref, dv_ref):
    h = _rms(x_ref[0], g_ref[...]).astype(BF16)

    def proj(c0, width):
        return _bdot(h, win_ref[:, c0:c0 + width])

    def store_heads(ref, z, n_heads):
        for hd in range(n_heads):
            ref[0, hd] = z[:, hd * HEAD_DIM:(hd + 1) * HEAD_DIM].astype(BF16)

    def store_vT_ext(ref, vT, n_heads):
        tk = ref.shape[3]
        pad = V_EXT - HEAD_DIM
        ones_row = (lax.broadcasted_iota(jnp.int32, (pad, tk), 0) == 0).astype(F32).astype(BF16)
        for t in range(ref.shape[1]):
            for hd in range(n_heads):
                ref[0, t, hd * V_EXT:hd * V_EXT + HEAD_DIM, :] = (
                    vT[hd * HEAD_DIM:(hd + 1) * HEAD_DIM, t * tk:(t + 1) * tk].astype(BF16))
                ref[0, t, hd * V_EXT + HEAD_DIM:(hd + 1) * V_EXT, :] = ones_row

    store_heads(aq_ref, proj(_C_AQ, GROUP_WIDTH), GROUP_HEADS)
    store_heads(ak_ref, proj(_C_AK, GROUP_WIDTH), GROUP_HEADS)
    store_heads(av_ref, proj(_C_AV, GROUP_WIDTH), GROUP_HEADS)

    cq = _rms(proj(_C_BCQ, MLA_Q_RANK), qg_ref[...]).astype(BF16)
    qb = _bdot(cq, wuq_ref[...]) * ((MLA_NOPE + MLA_ROPE) ** -0.5 * LOG2E)
    nope_w = GROUP_HEADS * MLA_NOPE
    q_nope_T = qb[:, :nope_w].T
    q_pe_T = qb[:, nope_w:].T
    half = MLA_ROPE // 2
    cos_m = tabc_ref[HEAD_DIM:HEAD_DIM + half]
    sin_m = tabc_ref[HEAD_DIM + half:HEAD_DIM + 2 * half]
    for hd in range(GROUP_HEADS):
        r0 = hd * LANES
        x1 = q_pe_T[hd * MLA_ROPE:hd * MLA_ROPE + half]
        x2 = q_pe_T[hd * MLA_ROPE + half:(hd + 1) * MLA_ROPE]
        bqT_ref[0, r0:r0 + MLA_NOPE, :] = q_nope_T[hd * MLA_NOPE:(hd + 1) * MLA_NOPE].astype(BF16)
        bqT_ref[0, r0 + MLA_NOPE:r0 + MLA_NOPE + half, :] = (x1 * cos_m - x2 * sin_m).astype(BF16)
        bqT_ref[0, r0 + MLA_NOPE + half:r0 + MLA_NOPE + MLA_ROPE, :] = (
            x2 * cos_m + x1 * sin_m).astype(BF16)
        bqT_ref[0, r0 + MLA_NOPE + MLA_ROPE:r0 + LANES, :] = jnp.zeros(
            (LANES - MLA_NOPE - MLA_ROPE, qb.shape[0]), BF16)
    assert (_C_BKR, _C_CV, _C_DV) == (_C_BCKV + LANES, _C_CK + LANES, _C_DK + LANES)
    b_pair = proj(_C_BCKV, _PAIR)
    kpe = _rope_lanes(b_pair[:, LANES:], tabb_ref, MLA_ROPE // 2)
    ckv = _rms(b_pair[:, :LANES], kvg_ref[...]).astype(BF16)
    kn = _bdot(ckv, wuk_ref[...])
    for hd in range(GROUP_HEADS):
        blk = slice(hd * LANES, (hd + 1) * LANES)
        bk_ref[0, :, blk] = (kn[:, blk] + kpe).astype(BF16)
    store_vT_ext(bvT_ref, _bdot(ckv, wuv_ref[...]).T, GROUP_HEADS)

    def norm_rope_T(blk, gain_col):
        ms = jnp.mean(blk * blk, axis=0, keepdims=True)
        blk = blk * lax.rsqrt(ms + NORM_EPS) * gain_col
        q = HEAD_DIM // 4
        cr, sr = tabc_ref[0:q], tabc_ref[q:2 * q]
        cc, sc = tabc_ref[2 * q:3 * q], tabc_ref[3 * q:4 * q]
        x1, x2, x3, x4 = blk[0:q], blk[q:2 * q], blk[2 * q:3 * q], blk[3 * q:4 * q]
        return jnp.concatenate([x1 * cr - x2 * sr, x2 * cr + x1 * sr,
                                x3 * cc - x4 * sc, x4 * cc + x3 * sc], axis=0)

    cqT = proj(_C_CQ, GROUP_WIDTH).T
    for hd in range(GROUP_HEADS):
        rows = slice(hd * HEAD_DIM, (hd + 1) * HEAD_DIM)
        cqT_ref[0, rows, :] = norm_rope_T(cqT[rows], cqg_ref[...]).astype(BF16)
    c_pair = proj(_C_CK, _PAIR)
    ckT = c_pair[:, :LANES].T
    ck = jnp.concatenate(
        [norm_rope_T(ckT[hd * HEAD_DIM:(hd + 1) * HEAD_DIM], ckg_ref[...])
         for hd in range(GROUP_KV_HEADS)], axis=0).T
    store_heads(ck_ref, ck, GROUP_KV_HEADS)
    store_vT_ext(cvT_ref, c_pair[:, LANES:].T, GROUP_KV_HEADS)

    dq = proj(_C_DQ, GROUP_WIDTH)
    dq = jnp.concatenate([_rope_lanes(dq[:, j * LANES:(j + 1) * LANES], tabd_ref, HEAD_DIM // 2)
                          for j in range(GROUP_WIDTH // LANES)], axis=1)
    store_heads(dq_ref, dq, GROUP_HEADS)
    d_pair = proj(_C_DK, _PAIR)
    dk = _rope_lanes(d_pair[:, :LANES], tabd_ref, HEAD_DIM // 2)
    store_heads(dk_ref, dk, GROUP_KV_HEADS)
    store_heads(dv_ref, d_pair[:, LANES:], GROUP_KV_HEADS)


def _const_spec(shape):
    n = len(shape)
    return pl.BlockSpec(shape, lambda *_: (0,) * n, pipeline_mode=pl.Buffered(1))


def _projection(x, lw, tabs, tm):
    b, s, _ = x.shape
    nt = s // tm
    nk = s // FLASH_TK
    assert tm % FLASH_TK == 0
    head_q = jax.ShapeDtypeStruct((b, GROUP_HEADS, s, HEAD_DIM), BF16)
    head_kv = jax.ShapeDtypeStruct((b, GROUP_KV_HEADS, s, HEAD_DIM), BF16)
    out_shape = (
        head_q, head_q, head_q,
        jax.ShapeDtypeStruct((b, GROUP_HEADS * LANES, s), BF16),
        jax.ShapeDtypeStruct((b, s, GROUP_HEADS * LANES), BF16),
        jax.ShapeDtypeStruct((b, nk, GROUP_HEADS * V_EXT, FLASH_TK), BF16),
        jax.ShapeDtypeStruct((b, GROUP_WIDTH, s), BF16),
        head_kv,
        jax.ShapeDtypeStruct((b, nk, GROUP_KV_HEADS * V_EXT, FLASH_TK), BF16),
        head_q, head_kv, head_kv,
    )
    hq_spec = pl.BlockSpec((1, GROUP_HEADS, tm, HEAD_DIM), lambda bi, i: (bi, 0, i, 0))
    hkv_spec = pl.BlockSpec((1, GROUP_KV_HEADS, tm, HEAD_DIM), lambda bi, i: (bi, 0, i, 0))
    out_specs = (
        hq_spec, hq_spec, hq_spec,
        pl.BlockSpec((1, GROUP_HEADS * LANES, tm), lambda bi, i: (bi, 0, i)),
        pl.BlockSpec((1, tm, GROUP_HEADS * LANES), lambda bi, i: (bi, i, 0)),
        pl.BlockSpec((1, tm // FLASH_TK, GROUP_HEADS * V_EXT, FLASH_TK),
                     lambda bi, i: (bi, i, 0, 0)),
        pl.BlockSpec((1, GROUP_WIDTH, tm), lambda bi, i: (bi, 0, i)),
        hkv_spec,
        pl.BlockSpec((1, tm // FLASH_TK, GROUP_KV_HEADS * V_EXT, FLASH_TK),
                     lambda bi, i: (bi, i, 0, 0)),
        hq_spec, hkv_spec, hkv_spec,
    )
    in_specs = [
        pl.BlockSpec((1, tm, D_MODEL), lambda bi, i: (bi, i, 0)),
        _const_spec((1, D_MODEL)),
        _const_spec((D_MODEL, _IN_COLS_PADDED)),
        _const_spec((1, MLA_Q_RANK)),
        _const_spec((MLA_Q_RANK, GROUP_HEADS * (MLA_NOPE + MLA_ROPE))),
        _const_spec((1, MLA_KV_RANK)),
        _const_spec((MLA_KV_RANK, GROUP_HEADS * LANES)),
        _const_spec((MLA_KV_RANK, GROUP_WIDTH)),
        _const_spec((HEAD_DIM, 1)),
        _const_spec((HEAD_DIM, 1)),
        pl.BlockSpec((3, tm, LANES), lambda bi, i: (0, i, 0)),
        pl.BlockSpec((3, tm, LANES), lambda bi, i: (0, i, 0)),
        pl.BlockSpec((HEAD_DIM + MLA_ROPE, tm), lambda bi, i: (0, i)),
    ]
    return pl.pallas_call(
        _proj_kernel,
        grid=(b, nt),
        in_specs=in_specs,
        out_specs=out_specs,
        out_shape=out_shape,
        compiler_params=pltpu.CompilerParams(
            dimension_semantics=("parallel", "parallel"), vmem_limit_bytes=VMEM_LIMIT),
        name="projection",
    )(x, lw["pre_gain"], lw["w_in"], lw["q_gain"], lw["w_uq"], lw["kv_gain"], lw["w_uk"],
      lw["w_uv"], lw["cq_gain"], lw["ck_gain"], tabs["mla"], tabs["full"], tabs["axial"])


def _flash_kernel(qT_ref, k_ref, vT_ref, oT_ref, s_even, s_odd, *, tk, n_chunks, n_streams):
    tq = qT_ref.shape[2] // n_streams
    dv_ext = vT_ref.shape[2]

    def produce(s_ref, c):
        k = k_ref[0, pl.ds(pl.multiple_of(c * tk, tk), tk), :]
        maxes = []
        for st in range(n_streams):
            s = _bdot(k, qT_ref[0, :, st * tq:(st + 1) * tq])
            s_ref[st] = s
            maxes.append(jnp.max(s, axis=0, keepdims=True))
        return tuple(maxes)

    def consume(s_ref, chunk_max, c, carry):
        vT = vT_ref[0, c]
        out = []
        for st, (m, acc) in enumerate(carry):
            m_new = jnp.maximum(m, chunk_max[st])
            p = jnp.exp2(s_ref[st] - m_new).astype(BF16)
            acc = jnp.exp2(m - m_new) * acc + _bdot(vT, p)
            out.append((m_new, acc))
        return tuple(out)

    def pair(jj, state):
        carry, max_even = state
        max_odd = produce(s_odd, 2 * jj + 1)
        carry = consume(s_even, max_even, 2 * jj, carry)
        max_even = produce(s_even, 2 * jj + 2)
        return consume(s_odd, max_odd, 2 * jj + 1, carry), max_even

    def body(t, state):
        for u in range(FLASH_PAIRS_PER_TRIP):
            state = pair(t * FLASH_PAIRS_PER_TRIP + u, state)
        return state

    carry = tuple((jnp.full((1, tq), -jnp.inf, F32), jnp.zeros((dv_ext, tq), F32))
                  for _ in range(n_streams))
    max_even = produce(s_even, 0)
    n_pairs = n_chunks // 2 - 1
    n_trips = n_pairs // FLASH_PAIRS_PER_TRIP
    state = lax.fori_loop(0, n_trips, body, (carry, max_even))
    for jj in range(n_trips * FLASH_PAIRS_PER_TRIP, n_pairs):
        state = pair(jj, state)
    carry, max_even = state
    max_odd = produce(s_odd, n_chunks - 1)
    carry = consume(s_even, max_even, n_chunks - 2, carry)
    carry = consume(s_odd, max_odd, n_chunks - 1, carry)
    for st, (_, acc) in enumerate(carry):
        oT_ref[0, :, st * tq:(st + 1) * tq] = (
            acc[:HEAD_DIM] / acc[HEAD_DIM:HEAD_DIM + 1]).astype(oT_ref.dtype)


def _flash(qT, k, vT, *, n_heads, n_kv, dk, k_head_major, tq, n_streams):
    b, _, s = qT.shape
    n_chunks, tk = vT.shape[1], vT.shape[3]
    assert n_chunks % 2 == 0 and s % tq == 0
    dv = HEAD_DIM
    rep = n_heads // n_kv
    if k_head_major:
        k_spec = pl.BlockSpec((None, 1, s, dk), lambda bi, h, i: (bi, h // rep, 0, 0))
    else:
        k_spec = pl.BlockSpec((1, s, dk), lambda bi, h, i: (bi, 0, h // rep))
    return pl.pallas_call(
        functools.partial(_flash_kernel, tk=tk, n_chunks=n_chunks, n_streams=n_streams),
        grid=(b, n_heads, s // tq),
        in_specs=[
            pl.BlockSpec((1, dk, tq), lambda bi, h, i: (bi, h, i)),
            k_spec,
            pl.BlockSpec((1, n_chunks, V_EXT, tk), lambda bi, h, i: (bi, 0, h // rep, 0)),
        ],
        out_specs=pl.BlockSpec((1, dv, tq), lambda bi, h, i: (bi, h, i)),
        out_shape=jax.ShapeDtypeStruct((b, n_heads * dv, s), BF16),
        scratch_shapes=[pltpu.VMEM((n_streams, tk, tq // n_streams), F32)] * 2,
        compiler_params=pltpu.CompilerParams(
            dimension_semantics=("parallel", "parallel", "parallel"),
            vmem_limit_bytes=VMEM_LIMIT),
        name="dense_attention",
    )(qT, k, vT)


def _na_kernel(q_ref, kp_ref, kc_ref, kn_ref, vp_ref, vc_ref, vn_ref, bias_ref, o_ref,
               k_win, v_win, *, n_rows):
    i = pl.program_id(1)
    blk = NA_ROWS_PER_STEP * GRID_W
    halo = NA_HALO_ROWS * GRID_W
    band = NA_WIN_ROWS * GRID_W
    for r0, r1, kr, vr in ((0, halo, kp_ref, vp_ref), (halo, halo + blk, kc_ref, vc_ref),
                           (halo + blk, 2 * halo + blk, kn_ref, vn_ref)):
        k_win[:, r0:r1, :] = kr[0]
        v_win[:, r0:r1, :] = vr[0]
    win_row0 = i * NA_ROWS_PER_STEP - NA_HALO_ROWS
    heads = range(GROUP_HEADS)
    for j0 in range(0, NA_ROWS_PER_STEP, NA_ROWS_INTERLEAVED):
        rows = range(j0, j0 + NA_ROWS_INTERLEAVED)
        offs, ds = {}, {}
        for j in rows:
            r = i * NA_ROWS_PER_STEP + j
            rs = jnp.clip(r - NA_WIN_ROWS // 2, 0, n_rows - NA_WIN_ROWS)
            offs[j] = pl.multiple_of((rs - win_row0) * GRID_W, GRID_W)
            ds[j] = r - rs
        ss = {(j, hd): lax.dot_general(q_ref[0, hd, j * GRID_W:(j + 1) * GRID_W, :],
                                       k_win[hd, pl.ds(offs[j], band), :],
                                       (((1,), (1,)), ((), ())), preferred_element_type=F32)
              + bias_ref[hd, ds[j]] for j in rows for hd in heads}
        ps = {key: jnp.exp(s - jnp.max(s, axis=-1, keepdims=True)) for key, s in ss.items()}
        for j in rows:
            os = [_bdot(ps[j, hd].astype(BF16), v_win[hd, pl.ds(offs[j], band), :])
                  / jnp.sum(ps[j, hd], axis=-1, keepdims=True) for hd in heads]
            o_ref[0, j * GRID_W:(j + 1) * GRID_W, :] = jnp.concatenate(os, axis=1).astype(o_ref.dtype)


def _neighbourhood(q, k, v, bias):
    b, nh, s, hd = q.shape
    blk = NA_ROWS_PER_STEP * GRID_W
    nb = s // blk
    n_rows = s // GRID_W
    halo = NA_HALO_ROWS * GRID_W
    ratio = blk // halo
    n_halo = s // halo
    cur = pl.BlockSpec((1, nh, blk, hd), lambda bi, i: (bi, 0, i, 0))
    prev = pl.BlockSpec((1, nh, halo, hd), lambda bi, i: (bi, 0, jnp.maximum(i * ratio - 1, 0), 0))
    nxt = pl.BlockSpec((1, nh, halo, hd),
                       lambda bi, i: (bi, 0, jnp.minimum((i + 1) * ratio, n_halo - 1), 0))
    return pl.pallas_call(
        functools.partial(_na_kernel, n_rows=n_rows),
        grid=(b, nb),
        in_specs=[cur, prev, cur, nxt, prev, cur, nxt, _const_spec(bias.shape)],
        out_specs=pl.BlockSpec((1, blk, nh * hd), lambda bi, i: (bi, i, 0)),
        out_shape=jax.ShapeDtypeStruct((b, s, nh * hd), BF16),
        scratch_shapes=[pltpu.VMEM((nh, blk + 2 * halo, hd), BF16)] * 2,
        compiler_params=pltpu.CompilerParams(
            dimension_semantics=("parallel", "parallel"), vmem_limit_bytes=VMEM_LIMIT),
        name="neighbourhood_attention",
    )(q, k, k, k, v, v, v, bias)


def _na_bias_table(rpb):
    c = np.arange(GRID_W)[:, None]
    kc = np.arange(GRID_W)[None, :]
    cs = np.clip(c - NA_WIN_COLS // 2, 0, GRID_W - NA_WIN_COLS)
    valid = (kc >= cs) & (kc < cs + NA_WIN_COLS)
    col_off = kc - c + (NA_WIN_COLS - 1)
    n_off = 2 * NA_WIN_COLS - 1
    select = (valid[:, :, None] & (col_off[:, :, None] == np.arange(n_off))).astype(np.float32)
    x = jnp.einsum("hro,cko->hrck", rpb.astype(F32), jnp.asarray(select),
                   precision=lax.Precision.HIGHEST)
    x = jnp.where(valid[None, None], x, MASK_VALUE)
    t = jnp.stack([x[:, NA_WIN_ROWS - 1 - d:2 * NA_WIN_ROWS - 1 - d] for d in range(NA_WIN_ROWS)],
                  axis=1)
    t = t.transpose(0, 1, 3, 2, 4)
    return t.reshape(rpb.shape[0], NA_WIN_ROWS, GRID_W, NA_WIN_ROWS * GRID_W)


def _sw_kernel(sink_ref, q_ref, kp_ref, kc_ref, kn_ref, vp_ref, vc_ref, vn_ref, o_ref, *, seq):
    i = pl.program_id(1)
    tq = q_ref.shape[2]
    span = tq + 2 * SW_WINDOW
    t0 = i * tq
    row = lax.broadcasted_iota(jnp.int32, (tq, span), 0)
    col = lax.broadcasted_iota(jnp.int32, (tq, span), 1)
    kpos = col + (t0 - SW_WINDOW)
    rel = col - row
    valid = (rel >= 0) & (rel <= 2 * SW_WINDOW) & (kpos >= 0) & (kpos < seq)
    nt = (((1,), (1,)), ((), ()))
    rep = GROUP_HEADS // GROUP_KV_HEADS
    heads = range(GROUP_HEADS)
    ss = [jnp.where(valid, jnp.concatenate(
        [lax.dot_general(q_ref[0, hd], kr[0, hd // rep], nt, preferred_element_type=F32)
         for kr in (kp_ref, kc_ref, kn_ref)], axis=1), MASK_VALUE) for hd in heads]
    ms = [jnp.maximum(jnp.max(ss[hd], axis=-1, keepdims=True), sink_ref[hd]) for hd in heads]
    ps = [jnp.exp(ss[hd] - ms[hd]) for hd in heads]
    denoms = [jnp.sum(ps[hd], axis=-1, keepdims=True) + jnp.exp(sink_ref[hd] - ms[hd])
              for hd in heads]
    os = []
    for hd in heads:
        g = hd // rep
        pb = ps[hd].astype(BF16)
        o = (_bdot(pb[:, :SW_WINDOW], vp_ref[0, g])
             + _bdot(pb[:, SW_WINDOW:SW_WINDOW + tq], vc_ref[0, g])
             + _bdot(pb[:, SW_WINDOW + tq:], vn_ref[0, g]))
        os.append(o / denoms[hd])
    o_ref[0] = jnp.concatenate(os, axis=1).astype(o_ref.dtype)


def _sliding_window(q, k, v, sink, tq):
    b, nh, s, hd = q.shape
    nkv = k.shape[1]
    nb = s // tq
    r = tq // SW_WINDOW
    n_small = s // SW_WINDOW
    cur = pl.BlockSpec((1, nkv, tq, hd), lambda bi, i: (bi, 0, i, 0))
    prev = pl.BlockSpec((1, nkv, SW_WINDOW, hd), lambda bi, i: (bi, 0, jnp.maximum(i * r - 1, 0), 0))
    nxt = pl.BlockSpec((1, nkv, SW_WINDOW, hd),
                       lambda bi, i: (bi, 0, jnp.minimum((i + 1) * r, n_small - 1), 0))
    return pl.pallas_call(
        functools.partial(_sw_kernel, seq=s),
        grid=(b, nb),
        in_specs=[pl.BlockSpec(memory_space=pltpu.SMEM),
                  pl.BlockSpec((1, nh, tq, hd), lambda bi, i: (bi, 0, i, 0)),
                  prev, cur, nxt, prev, cur, nxt],
        out_specs=pl.BlockSpec((1, tq, nh * hd), lambda bi, i: (bi, i, 0)),
        out_shape=jax.ShapeDtypeStruct((b, s, nh * hd), BF16),
        compiler_params=pltpu.CompilerParams(
            dimension_semantics=("parallel", "parallel"), vmem_limit_bytes=VMEM_LIMIT),
        name="sliding_window_attention",
    )(sink, q, k, k, k, v, v, v)


def _out_kernel(x_ref, oa_ref, obT_ref, ocT_ref, od_ref, w_ref, g_ref, o_ref):
    mixed_in = jnp.concatenate(
        [oa_ref[0], obT_ref[0].astype(F32).T.astype(BF16), ocT_ref[0].astype(F32).T.astype(BF16),
         od_ref[0]], axis=1)
    mixed = _bdot(mixed_in, w_ref[...])
    o_ref[0] = x_ref[0] + _rms(mixed, g_ref[...])


def _out_projection(x, o_a, o_bT, o_cT, o_d, w_out, gain, tm):
    b, s, _ = x.shape
    tok = lambda w: pl.BlockSpec((1, tm, w), lambda bi, i: (bi, i, 0))
    feat = pl.BlockSpec((1, GROUP_WIDTH, tm), lambda bi, i: (bi, 0, i))
    return pl.pallas_call(
        _out_kernel,
        grid=(b, s // tm),
        in_specs=[tok(D_MODEL), tok(GROUP_WIDTH), feat, feat, tok(GROUP_WIDTH),
                  _const_spec((D_MODEL, D_MODEL)), _const_spec((1, D_MODEL))],
        out_specs=tok(D_MODEL),
        out_shape=jax.ShapeDtypeStruct(x.shape, F32),
        compiler_params=pltpu.CompilerParams(
            dimension_semantics=("parallel", "parallel"), vmem_limit_bytes=VMEM_LIMIT),
        name="out_projection",
    )(x, o_a, o_bT, o_cT, o_d, w_out, gain)


FFN_HALO = 8


def _ffn_kernel(x_ref, xp_ref, xn_ref, g_ref, wup_ref, cw_ref, cb_ref, wd_ref, pg_ref, o_ref,
                h_scr, acc_scr, u_even, u_odd, *, fc):
    n_chunks = D_FF // fc

    def cols(ref, c, base):
        return ref[:, pl.ds(pl.multiple_of(base + c * fc, LANES), fc)]

    i = pl.program_id(1)
    n_tiles = pl.num_programs(1)
    tm = x_ref.shape[1]
    ext = tm + 2 * FFN_HALO
    g = g_ref[...]
    hp = _rms(xp_ref[0], g) * (i > 0).astype(F32)
    hn = _rms(xn_ref[0], g) * (i < n_tiles - 1).astype(F32)
    h_scr[...] = jnp.concatenate([hp, _rms(x_ref[0], g), hn], axis=0).astype(BF16)
    acc_scr[...] = jnp.zeros_like(acc_scr)

    def produce(u_ref, c):
        hh = h_scr[...]
        u_ref[0] = _bdot(hh, cols(wup_ref, c, 0))
        u_ref[1] = _bdot(hh, cols(wup_ref, c, D_FF))

    def conv(u_ref, cw, cb):
        lo = FFN_HALO - 1
        return (cb + u_ref[lo:lo + tm] * cw[0:1] + u_ref[lo + 1:lo + 1 + tm] * cw[1:2]
                + u_ref[lo + 2:lo + 2 + tm] * cw[2:3])

    def consume(u_ref, c):
        gate = conv(u_ref.at[0], cols(cw_ref, c, 0), cols(cb_ref, c, 0))
        val = conv(u_ref.at[1], cols(cw_ref, c, D_FF), cols(cb_ref, c, D_FF))
        act = jax.nn.gelu(gate, approximate=True) * val
        acc_scr[...] += _bdot(act.astype(BF16), wd_ref[pl.ds(pl.multiple_of(c * fc, fc), fc), :])

    def pair(jj):
        produce(u_odd, 2 * jj + 1)
        consume(u_even, 2 * jj)
        produce(u_even, 2 * jj + 2)
        consume(u_odd, 2 * jj + 1)

    def body(t, carry):
        for u in range(FFN_PAIRS_PER_TRIP):
            pair(t * FFN_PAIRS_PER_TRIP + u)
        return carry

    n_pairs = (n_chunks - 1) // 2
    tail = n_chunks - 2 * n_pairs
    n_trips = n_pairs // FFN_PAIRS_PER_TRIP
    produce(u_even, 0)
    lax.fori_loop(0, n_trips, body, 0)
    for jj in range(n_trips * FFN_PAIRS_PER_TRIP, n_pairs):
        pair(jj)
    if tail == 2:
        produce(u_odd, n_chunks - 1)
    consume(u_even, 2 * n_pairs)
    if tail == 2:
        consume(u_odd, n_chunks - 1)
    o_ref[0] = x_ref[0] + _rms(acc_scr[...], pg_ref[...])


def _layer_spec(stacked, l):
    shape = stacked.shape[1:]
    return pl.BlockSpec((None,) + shape, lambda *_: (l,) + (0,) * len(shape),
                        pipeline_mode=pl.Buffered(1))


def _ffn(x, lw, mlp, l, tm):
    b, s, _ = x.shape
    fc = FF_CHUNK
    assert D_FF % fc == 0 and fc % LANES == 0
    r = tm // FFN_HALO
    n_halo = s // FFN_HALO
    tile = pl.BlockSpec((1, tm, D_MODEL), lambda bi, i: (bi, i, 0))
    prev = pl.BlockSpec((1, FFN_HALO, D_MODEL), lambda bi, i: (bi, jnp.maximum(i * r - 1, 0), 0))
    nxt = pl.BlockSpec((1, FFN_HALO, D_MODEL),
                       lambda bi, i: (bi, jnp.minimum((i + 1) * r, n_halo - 1), 0))
    return pl.pallas_call(
        functools.partial(_ffn_kernel, fc=fc),
        grid=(b, s // tm),
        in_specs=[tile, prev, nxt, _const_spec((1, D_MODEL)),
                  _layer_spec(mlp["w_up"], l), _layer_spec(mlp["conv_w"], l),
                  _layer_spec(mlp["conv_b"], l), _layer_spec(mlp["w_down"], l),
                  _const_spec((1, D_MODEL))],
        out_specs=tile,
        out_shape=jax.ShapeDtypeStruct(x.shape, F32),
        scratch_shapes=[pltpu.VMEM((tm + 2 * FFN_HALO, D_MODEL), BF16),
                        pltpu.VMEM((tm, D_MODEL), F32),
                        pltpu.VMEM((2, tm + 2 * FFN_HALO, fc), F32),
                        pltpu.VMEM((2, tm + 2 * FFN_HALO, fc), F32)],
        compiler_params=pltpu.CompilerParams(
            dimension_semantics=("parallel", "parallel"), vmem_limit_bytes=VMEM_LIMIT),
        name="conv_mlp",
    )(x, x, x, lw["ffn_pre_gain"], mlp["w_up"], mlp["conv_w"], mlp["conv_b"], mlp["w_down"],
      lw["ffn_post_gain"])


def _rope_tables(s):
    t = jnp.arange(s)

    def angles(pos, dim):
        inv = ROPE_THETA ** (-jnp.arange(0, dim, 2, dtype=F32) / dim)
        return pos.astype(F32)[:, None] * inv[None, :]

    def lane_table(ang, lead, trail, reps):
        half = ang.shape[1]
        cos, sin, zero = jnp.cos(ang), jnp.sin(ang), jnp.zeros_like(ang)
        one = lambda n: jnp.ones((s, n), F32)
        nul = lambda n: jnp.zeros((s, n), F32)
        c = jnp.concatenate([one(lead)] + [cos, cos] * reps + [one(trail)], axis=1)
        lo = jnp.concatenate([nul(lead)] + [-sin, zero] * reps + [nul(trail)], axis=1)
        hi = jnp.concatenate([nul(lead)] + [zero, sin] * reps + [nul(trail)], axis=1)
        assert c.shape[1] == LANES and 2 * half * reps + lead + trail == LANES
        return jnp.stack([c, lo, hi])

    ang_row = angles(t // GRID_W, HEAD_DIM // 2)
    ang_col = angles(t % GRID_W, HEAD_DIM // 2)
    ang_mla = angles(t, MLA_ROPE)
    return {
        "mla": lane_table(ang_mla, MLA_NOPE, LANES - MLA_NOPE - MLA_ROPE, 1),
        "full": lane_table(angles(t, HEAD_DIM), 0, 0, LANES // HEAD_DIM),
        "axial": jnp.concatenate([jnp.cos(ang_row), jnp.sin(ang_row),
                                  jnp.cos(ang_col), jnp.sin(ang_col),
                                  jnp.cos(ang_mla), jnp.sin(ang_mla)], axis=1).T,
    }


def _layer_weights(l, mix_pre_gain, w_in, na_rpb, mla_q_gain, mla_w_uq, mla_kv_gain, mla_w_ukv,
                   ax_q_gain, ax_k_gain, sw_sink, w_out, mix_post_gain, ffn_pre_gain, w_up,
                   conv_w, conv_b, w_down, ffn_post_gain):
    gw, kvw = GROUP_WIDTH, GROUP_KV_HEADS * HEAD_DIM
    sizes = (gw, gw, gw, MLA_Q_RANK, MLA_KV_RANK, MLA_ROPE, gw, kvw, kvw, gw, kvw, kvw)
    bounds = np.cumsum((0,) + sizes)
    (a_q, a_k, a_v, b_cq, b_ckv, b_kr, c_q, c_k, c_v, d_q, d_k, d_v) = [
        w_in[l][:, bounds[j]:bounds[j + 1]] for j in range(len(sizes))]
    scale = HEAD_DIM ** -0.5
    zeros = lambda n: jnp.zeros((D_MODEL, n), F32)
    kr_block = jnp.concatenate([zeros(MLA_NOPE), b_kr, zeros(LANES - MLA_NOPE - MLA_ROPE)], axis=1)
    w_in_r = jnp.concatenate([a_q * scale, a_k, a_v, b_cq, b_ckv, kr_block, c_q, c_k, c_v,
                              d_q * scale, d_k, d_v], axis=1)
    assert w_in_r.shape[1] == _IN_COLS_PADDED

    uq = mla_w_uq[l].reshape(MLA_Q_RANK, GROUP_HEADS, MLA_NOPE + MLA_ROPE)
    uq = jnp.concatenate([uq[:, :, :MLA_NOPE].reshape(MLA_Q_RANK, -1),
                          uq[:, :, MLA_NOPE:].reshape(MLA_Q_RANK, -1)], axis=1)
    ukv = mla_w_ukv[l].reshape(MLA_KV_RANK, GROUP_HEADS, MLA_NOPE + HEAD_DIM)
    uk = jnp.pad(ukv[:, :, :MLA_NOPE], ((0, 0), (0, 0), (0, LANES - MLA_NOPE)))
    uv = ukv[:, :, MLA_NOPE:]

    row = lambda v: v[None, :].astype(F32)
    return {
        "pre_gain": row(mix_pre_gain[l]),
        "w_in": w_in_r.astype(BF16),
        "q_gain": row(mla_q_gain[l]),
        "w_uq": uq.astype(BF16),
        "kv_gain": row(mla_kv_gain[l]),
        "w_uk": uk.reshape(MLA_KV_RANK, GROUP_HEADS * LANES).astype(BF16),
        "w_uv": uv.reshape(MLA_KV_RANK, GROUP_WIDTH).astype(BF16),
        "cq_gain": (ax_q_gain[l] * (scale * LOG2E))[:, None].astype(F32),
        "ck_gain": ax_k_gain[l][:, None].astype(F32),
        "na_bias": _na_bias_table(na_rpb[l]),
        "sink": sw_sink[l].astype(F32),
        "w_out": w_out[l].astype(BF16),
        "post_gain": row(mix_post_gain[l]),
        "ffn_pre_gain": row(ffn_pre_gain[l]),
        "ffn_post_gain": row(ffn_post_gain[l]),
    }


def kernel(x, mix_pre_gain, w_in, na_rpb, mla_q_gain, mla_w_uq, mla_kv_gain, mla_w_ukv, ax_q_gain,
           ax_k_gain, sw_sink, w_out, mix_post_gain, ffn_pre_gain, w_up, conv_w, conv_b, w_down,
           ffn_post_gain):
    b, s, d = x.shape
    assert d == D_MODEL and s % max(TOKEN_TILE, FLASH_TQ, NA_ROWS_PER_STEP * GRID_W, SW_TQ) == 0
    assert s // GRID_W >= NA_WIN_ROWS
    params = (mix_pre_gain, w_in, na_rpb, mla_q_gain, mla_w_uq, mla_kv_gain, mla_w_ukv, ax_q_gain,
              ax_k_gain, sw_sink, w_out, mix_post_gain, ffn_pre_gain, w_up, conv_w, conv_b, w_down,
              ffn_post_gain)
    tabs = _rope_tables(s)
    mlp = {"w_up": w_up.astype(BF16), "w_down": w_down.astype(BF16),
           "conv_w": conv_w.astype(F32), "conv_b": conv_b.astype(F32)[:, None, :]}
    for l in range(w_in.shape[0]):
        lw = _layer_weights(l, *params)
        (a_q, a_k, a_v, b_qT, b_k, b_vT, c_qT, c_k, c_vT, d_q, d_k, d_v) = _projection(
            x, lw, tabs, TOKEN_TILE)
        o_a = _neighbourhood(a_q, a_k, a_v, lw["na_bias"])
        o_bT = _flash(b_qT, b_k, b_vT, n_heads=GROUP_HEADS, n_kv=GROUP_HEADS, dk=LANES,
                      k_head_major=False, tq=FLASH_TQ * FLASH_STREAMS, n_streams=FLASH_STREAMS)
        o_cT = _flash(c_qT, c_k, c_vT, n_heads=GROUP_HEADS, n_kv=GROUP_KV_HEADS, dk=HEAD_DIM,
                      k_head_major=True, tq=FLASH_TQ * FLASH_STREAMS, n_streams=FLASH_STREAMS)
        o_d = _sliding_window(d_q, d_k, d_v, lw["sink"], SW_TQ)
        x = _out_projection(x, o_a, o_bT, o_cT, o_d, lw["w_out"], lw["post_gain"], TOKEN_TILE)
        x = _ffn(x, lw, mlp, l, FFN_TILE)
    return x
```

```python
import functools
import math

import numpy as np
import jax
import jax.numpy as jnp
from jax import lax
from jax.experimental import pallas as pl
from jax.experimental.pallas import tpu as pltpu

F32 = jnp.float32
BF16 = jnp.bfloat16

D_MODEL = 1024
GRID_W = 64
HEAD_DIM = 64
GROUP_HEADS = 4
GROUP_KV_HEADS = 2
GROUP_WIDTH = GROUP_HEADS * HEAD_DIM
ROPE_THETA = 10000.0
NORM_EPS = 1e-6
MASK_VALUE = -1e30
NA_WIN_ROWS = 8
NA_WIN_COLS = 16
MLA_Q_RANK = 256
MLA_KV_RANK = 128
MLA_NOPE = 64
MLA_ROPE = 32
SW_WINDOW = 128
D_FF = 2816
LOG2E = math.log2(math.e)

LANES = 128
VMEM_LIMIT = 56 * 1024 * 1024

TOKEN_TILE = 1024
FLASH_TQ = 512
FLASH_TK = 256
FLASH_STREAMS = 2
FLASH_UNITS = 2
FLASH_PAIRS_PER_TRIP = 5
V_EXT = HEAD_DIM + 16
NA_ROWS_PER_STEP = 8
NA_HALO_ROWS = NA_WIN_ROWS // 2
NA_ROWS_INTERLEAVED = 4
SW_TQ = 256
FFN_TILE = 1024
FFN_PAIRS_PER_TRIP = 1
FF_CHUNK = 256


def _rms(x, gain):
    return x * lax.rsqrt(jnp.mean(x * x, axis=-1, keepdims=True) + NORM_EPS) * gain


def _rope_lanes(x, tab_ref, half):
    w = x.shape[-1]
    return (x * tab_ref[0] + pltpu.roll(x, w - half, 1) * tab_ref[1]
            + pltpu.roll(x, half, 1) * tab_ref[2])


_GELU_C = math.sqrt(2.0 / math.pi)


def _gelu_tanh(x):
    k1 = -2.0 * _GELU_C * LOG2E
    k3 = k1 * 0.044715
    return x / (1.0 + jnp.exp2(x * (k1 + k3 * (x * x))))


def _bdot(a, b):
    return jnp.dot(a, b, preferred_element_type=F32)


_C_AQ, _C_AK, _C_AV = 0, 256, 512
_C_BCQ, _C_BCKV, _C_BKR = 768, 1024, 1152
_C_CQ, _C_CK, _C_CV = 1280, 1536, 1664
_C_DQ, _C_DK, _C_DV = 1792, 2048, 2176
_IN_COLS_PADDED = 2304
_PAIR = 2 * LANES


def _proj_kernel(x_ref, g_ref, win_ref, qg_ref, wuq_ref, kvg_ref, wuk_ref, wuv_ref,
                 cqg_ref, ckg_ref, tabb_ref, tabd_ref, tabc_ref,
                 aq_ref, ak_ref, av_ref, bqT_ref, bk_ref, bvT_ref,
                 cqT_ref, ck_ref, cvT_ref, dq_ref, dk_ref, dv_ref):
    h = _rms(x_ref[0], g_ref[...]).astype(BF16)

    def proj(c0, width):
        return _bdot(h, win_ref[:, c0:c0 + width])

    def store_heads(ref, z, n_heads):
        for hd in range(n_heads):
            ref[0, hd] = z[:, hd * HEAD_DIM:(hd + 1) * HEAD_DIM].astype(BF16)

    def store_vT_ext(ref, vT, n_heads):
        tk = ref.shape[3]
        pad = V_EXT - HEAD_DIM
        ones_row = (lax.broadcasted_iota(jnp.int32, (pad, tk), 0) == 0).astype(F32).astype(BF16)
        for t in range(ref.shape[1]):
            for hd in range(n_heads):
                ref[0, t, hd * V_EXT:hd * V_EXT + HEAD_DIM, :] = (
                    vT[hd * HEAD_DIM:(hd + 1) * HEAD_DIM, t * tk:(t + 1) * tk].astype(BF16))
                ref[0, t, hd * V_EXT + HEAD_DIM:(hd + 1) * V_EXT, :] = ones_row

    store_heads(aq_ref, proj(_C_AQ, GROUP_WIDTH), GROUP_HEADS)
    store_heads(ak_ref, proj(_C_AK, GROUP_WIDTH), GROUP_HEADS)
    store_heads(av_ref, proj(_C_AV, GROUP_WIDTH), GROUP_HEADS)

    cq = _rms(proj(_C_BCQ, MLA_Q_RANK), qg_ref[...]).astype(BF16)
    qb = _bdot(cq, wuq_ref[...]) * ((MLA_NOPE + MLA_ROPE) ** -0.5 * LOG2E)
    nope_w = GROUP_HEADS * MLA_NOPE
    q_nope_T = qb[:, :nope_w].T
    q_pe_T = qb[:, nope_w:].T
    half = MLA_ROPE // 2
    cos_m = tabc_ref[HEAD_DIM:HEAD_DIM + half]
    sin_m = tabc_ref[HEAD_DIM + half:HEAD_DIM + 2 * half]
    for hd in range(GROUP_HEADS):
        r0 = hd * LANES
        x1 = q_pe_T[hd * MLA_ROPE:hd * MLA_ROPE + half]
        x2 = q_pe_T[hd * MLA_ROPE + half:(hd + 1) * MLA_ROPE]
        bqT_ref[0, r0:r0 + MLA_NOPE, :] = q_nope_T[hd * MLA_NOPE:(hd + 1) * MLA_NOPE].astype(BF16)
        bqT_ref[0, r0 + MLA_NOPE:r0 + MLA_NOPE + half, :] = (x1 * cos_m - x2 * sin_m).astype(BF16)
        bqT_ref[0, r0 + MLA_NOPE + half:r0 + MLA_NOPE + MLA_ROPE, :] = (
            x2 * cos_m + x1 * sin_m).astype(BF16)
        bqT_ref[0, r0 + MLA_NOPE + MLA_ROPE:r0 + LANES, :] = jnp.zeros(
            (LANES - MLA_NOPE - MLA_ROPE, qb.shape[0]), BF16)
    assert (_C_BKR, _C_CV, _C_DV) == (_C_BCKV + LANES, _C_CK + LANES, _C_DK + LANES)
    b_pair = proj(_C_BCKV, _PAIR)
    kpe = _rope_lanes(b_pair[:, LANES:], tabb_ref, MLA_ROPE // 2)
    ckv = _rms(b_pair[:, :LANES], kvg_ref[...]).astype(BF16)
    kn = _bdot(ckv, wuk_ref[...])
    for hd in range(GROUP_HEADS):
        blk = slice(hd * LANES, (hd + 1) * LANES)
        bk_ref[0, :, blk] = (kn[:, blk] + kpe).astype(BF16)
    store_vT_ext(bvT_ref, _bdot(ckv, wuv_ref[...]).T, GROUP_HEADS)

    def norm_rope_T(blk, gain_col):
        ms = jnp.mean(blk * blk, axis=0, keepdims=True)
        blk = blk * lax.rsqrt(ms + NORM_EPS) * gain_col
        q = HEAD_DIM // 4
        cr, sr = tabc_ref[0:q], tabc_ref[q:2 * q]
        cc, sc = tabc_ref[2 * q:3 * q], tabc_ref[3 * q:4 * q]
        x1, x2, x3, x4 = blk[0:q], blk[q:2 * q], blk[2 * q:3 * q], blk[3 * q:4 * q]
        return jnp.concatenate([x1 * cr - x2 * sr, x2 * cr + x1 * sr,
                                x3 * cc - x4 * sc, x4 * cc + x3 * sc], axis=0)

    cqT = proj(_C_CQ, GROUP_WIDTH).T
    for hd in range(GROUP_HEADS):
        rows = slice(hd * HEAD_DIM, (hd + 1) * HEAD_DIM)
        cqT_ref[0, rows, :] = norm_rope_T(cqT[rows], cqg_ref[...]).astype(BF16)
    c_pair = proj(_C_CK, _PAIR)
    ckT = c_pair[:, :LANES].T
    ck = jnp.concatenate(
        [norm_rope_T(ckT[hd * HEAD_DIM:(hd + 1) * HEAD_DIM], ckg_ref[...])
         for hd in range(GROUP_KV_HEADS)], axis=0).T
    store_heads(ck_ref, ck, GROUP_KV_HEADS)
    store_vT_ext(cvT_ref, c_pair[:, LANES:].T, GROUP_KV_HEADS)

    dq = proj(_C_DQ, GROUP_WIDTH)
    dq = jnp.concatenate([_rope_lanes(dq[:, j * LANES:(j + 1) * LANES], tabd_ref, HEAD_DIM // 2)
                          for j in range(GROUP_WIDTH // LANES)], axis=1)
    store_heads(dq_ref, dq, GROUP_HEADS)
    d_pair = proj(_C_DK, _PAIR)
    dk = _rope_lanes(d_pair[:, :LANES], tabd_ref, HEAD_DIM // 2)
    store_heads(dk_ref, dk, GROUP_KV_HEADS)
    store_heads(dv_ref, d_pair[:, LANES:], GROUP_KV_HEADS)


def _const_spec(shape):
    n = len(shape)
    return pl.BlockSpec(shape, lambda *_: (0,) * n, pipeline_mode=pl.Buffered(1))


def _projection(x, lw, tabs, tm):
    b, s, _ = x.shape
    nt = s // tm
    nk = s // FLASH_TK
    assert tm % FLASH_TK == 0
    head_q = jax.ShapeDtypeStruct((b, GROUP_HEADS, s, HEAD_DIM), BF16)
    head_kv = jax.ShapeDtypeStruct((b, GROUP_KV_HEADS, s, HEAD_DIM), BF16)
    out_shape = (
        head_q, head_q, head_q,
        jax.ShapeDtypeStruct((b, GROUP_HEADS * LANES, s), BF16),
        jax.ShapeDtypeStruct((b, s, GROUP_HEADS * LANES), BF16),
        jax.ShapeDtypeStruct((b, nk, GROUP_HEADS * V_EXT, FLASH_TK), BF16),
        jax.ShapeDtypeStruct((b, GROUP_WIDTH, s), BF16),
        head_kv,
        jax.ShapeDtypeStruct((b, nk, GROUP_KV_HEADS * V_EXT, FLASH_TK), BF16),
        head_q, head_kv, head_kv,
    )
    hq_spec = pl.BlockSpec((1, GROUP_HEADS, tm, HEAD_DIM), lambda bi, i: (bi, 0, i, 0))
    hkv_spec = pl.BlockSpec((1, GROUP_KV_HEADS, tm, HEAD_DIM), lambda bi, i: (bi, 0, i, 0))
    out_specs = (
        hq_spec, hq_spec, hq_spec,
        pl.BlockSpec((1, GROUP_HEADS * LANES, tm), lambda bi, i: (bi, 0, i)),
        pl.BlockSpec((1, tm, GROUP_HEADS * LANES), lambda bi, i: (bi, i, 0)),
        pl.BlockSpec((1, tm // FLASH_TK, GROUP_HEADS * V_EXT, FLASH_TK),
                     lambda bi, i: (bi, i, 0, 0)),
        pl.BlockSpec((1, GROUP_WIDTH, tm), lambda bi, i: (bi, 0, i)),
        hkv_spec,
        pl.BlockSpec((1, tm // FLASH_TK, GROUP_KV_HEADS * V_EXT, FLASH_TK),
                     lambda bi, i: (bi, i, 0, 0)),
        hq_spec, hkv_spec, hkv_spec,
    )
    in_specs = [
        pl.BlockSpec((1, tm, D_MODEL), lambda bi, i: (bi, i, 0)),
        _const_spec((1, D_MODEL)),
        _const_spec((D_MODEL, _IN_COLS_PADDED)),
        _const_spec((1, MLA_Q_RANK)),
        _const_spec((MLA_Q_RANK, GROUP_HEADS * (MLA_NOPE + MLA_ROPE))),
        _const_spec((1, MLA_KV_RANK)),
        _const_spec((MLA_KV_RANK, GROUP_HEADS * LANES)),
        _const_spec((MLA_KV_RANK, GROUP_WIDTH)),
        _const_spec((HEAD_DIM, 1)),
        _const_spec((HEAD_DIM, 1)),
        pl.BlockSpec((3, tm, LANES), lambda bi, i: (0, i, 0)),
        pl.BlockSpec((3, tm, LANES), lambda bi, i: (0, i, 0)),
        pl.BlockSpec((HEAD_DIM + MLA_ROPE, tm), lambda bi, i: (0, i)),
    ]
    return pl.pallas_call(
        _proj_kernel,
        grid=(b, nt),
        in_specs=in_specs,
        out_specs=out_specs,
        out_shape=out_shape,
        compiler_params=pltpu.CompilerParams(
            dimension_semantics=("parallel", "parallel"), vmem_limit_bytes=VMEM_LIMIT),
        name="projection",
    )(x, lw["pre_gain"], lw["w_in"], lw["q_gain"], lw["w_uq"], lw["kv_gain"], lw["w_uk"],
      lw["w_uv"], lw["cq_gain"], lw["ck_gain"], tabs["mla"], tabs["full"], tabs["axial"])


def _flash_kernel(qT_ref, k_ref, vT_ref, oT_ref, s_even, s_odd, *, tk, n_chunks, n_streams):
    tq = qT_ref.shape[2] // (n_streams * FLASH_UNITS)
    dv_ext = vT_ref.shape[2]
    for unit in range(FLASH_UNITS):
        _flash_unit(qT_ref, k_ref, vT_ref, oT_ref, s_even, s_odd, unit * n_streams * tq, tq, dv_ext,
                    tk=tk, n_chunks=n_chunks, n_streams=n_streams)


def _flash_unit(qT_ref, k_ref, vT_ref, oT_ref, s_even, s_odd, q0, tq, dv_ext, *, tk, n_chunks,
                n_streams):
    def produce(s_ref, c):
        k = k_ref[0, pl.ds(pl.multiple_of(c * tk, tk), tk), :]
        maxes = []
        for st in range(n_streams):
            s = _bdot(k, qT_ref[0, :, q0 + st * tq:q0 + (st + 1) * tq])
            s_ref[st] = s
            maxes.append(jnp.max(s, axis=0, keepdims=True))
        return tuple(maxes)

    def consume(s_ref, chunk_max, c, carry):
        vT = vT_ref[0, c]
        out = []
        for st, (m, acc) in enumerate(carry):
            m_new = jnp.maximum(m, chunk_max[st])
            p = jnp.exp2(s_ref[st] - m_new).astype(BF16)
            acc = jnp.exp2(m - m_new) * acc + _bdot(vT, p)
            out.append((m_new, acc))
        return tuple(out)

    def pair(jj, state):
        carry, max_even = state
        max_odd = produce(s_odd, 2 * jj + 1)
        carry = consume(s_even, max_even, 2 * jj, carry)
        max_even = produce(s_even, 2 * jj + 2)
        return consume(s_odd, max_odd, 2 * jj + 1, carry), max_even

    def body(t, state):
        for u in range(FLASH_PAIRS_PER_TRIP):
            state = pair(t * FLASH_PAIRS_PER_TRIP + u, state)
        return state

    carry = tuple((jnp.full((1, tq), -jnp.inf, F32), jnp.zeros((dv_ext, tq), F32))
                  for _ in range(n_streams))
    max_even = produce(s_even, 0)
    n_pairs = n_chunks // 2 - 1
    n_trips = n_pairs // FLASH_PAIRS_PER_TRIP
    state = lax.fori_loop(0, n_trips, body, (carry, max_even))
    for jj in range(n_trips * FLASH_PAIRS_PER_TRIP, n_pairs):
        state = pair(jj, state)
    carry, max_even = state
    max_odd = produce(s_odd, n_chunks - 1)
    carry = consume(s_even, max_even, n_chunks - 2, carry)
    carry = consume(s_odd, max_odd, n_chunks - 1, carry)
    for st, (_, acc) in enumerate(carry):
        oT_ref[0, :, q0 + st * tq:q0 + (st + 1) * tq] = (
            acc[:HEAD_DIM] / acc[HEAD_DIM:HEAD_DIM + 1]).astype(oT_ref.dtype)


def _flash(qT, k, vT, *, n_heads, n_kv, dk, k_head_major, tq, n_streams):
    b, _, s = qT.shape
    n_chunks, tk = vT.shape[1], vT.shape[3]
    assert n_chunks % 2 == 0 and s % tq == 0
    dv = HEAD_DIM
    rep = n_heads // n_kv
    if k_head_major:
        k_spec = pl.BlockSpec((None, 1, s, dk), lambda bi, h, i: (bi, h // rep, 0, 0))
    else:
        k_spec = pl.BlockSpec((1, s, dk), lambda bi, h, i: (bi, 0, h // rep))
    return pl.pallas_call(
        functools.partial(_flash_kernel, tk=tk, n_chunks=n_chunks, n_streams=n_streams),
        grid=(b, n_heads, s // tq),
        in_specs=[
            pl.BlockSpec((1, dk, tq), lambda bi, h, i: (bi, h, i)),
            k_spec,
            pl.BlockSpec((1, n_chunks, V_EXT, tk), lambda bi, h, i: (bi, 0, h // rep, 0)),
        ],
        out_specs=pl.BlockSpec((1, dv, tq), lambda bi, h, i: (bi, h, i)),
        out_shape=jax.ShapeDtypeStruct((b, n_heads * dv, s), BF16),
        scratch_shapes=[pltpu.VMEM((n_streams, tk, tq // (n_streams * FLASH_UNITS)), F32)] * 2,
        compiler_params=pltpu.CompilerParams(
            dimension_semantics=("parallel", "parallel", "parallel"),
            vmem_limit_bytes=VMEM_LIMIT),
        name="dense_attention",
    )(qT, k, vT)


def _na_kernel(q_ref, kp_ref, kc_ref, kn_ref, vp_ref, vc_ref, vn_ref, bias_ref, o_ref,
               k_win, v_win, *, n_rows):
    i = pl.program_id(1)
    blk = NA_ROWS_PER_STEP * GRID_W
    halo = NA_HALO_ROWS * GRID_W
    band = NA_WIN_ROWS * GRID_W
    for r0, r1, kr, vr in ((0, halo, kp_ref, vp_ref), (halo, halo + blk, kc_ref, vc_ref),
                           (halo + blk, 2 * halo + blk, kn_ref, vn_ref)):
        k_win[:, r0:r1, :] = kr[0]
        v_win[:, r0:r1, :] = vr[0]
    win_row0 = i * NA_ROWS_PER_STEP - NA_HALO_ROWS
    heads = range(GROUP_HEADS)
    for j0 in range(0, NA_ROWS_PER_STEP, NA_ROWS_INTERLEAVED):
        rows = range(j0, j0 + NA_ROWS_INTERLEAVED)
        offs, ds = {}, {}
        for j in rows:
            r = i * NA_ROWS_PER_STEP + j
            rs = jnp.clip(r - NA_WIN_ROWS // 2, 0, n_rows - NA_WIN_ROWS)
            offs[j] = pl.multiple_of((rs - win_row0) * GRID_W, GRID_W)
            ds[j] = r - rs
        ss = {(j, hd): lax.dot_general(q_ref[0, hd, j * GRID_W:(j + 1) * GRID_W, :],
                                       k_win[hd, pl.ds(offs[j], band), :],
                                       (((1,), (1,)), ((), ())), preferred_element_type=F32)
              + bias_ref[hd, ds[j]] for j in rows for hd in heads}
        ps = {key: jnp.exp(s - jnp.max(s, axis=-1, keepdims=True)) for key, s in ss.items()}
        for j in rows:
            os = [_bdot(ps[j, hd].astype(BF16), v_win[hd, pl.ds(offs[j], band), :])
                  / jnp.sum(ps[j, hd], axis=-1, keepdims=True) for hd in heads]
            o_ref[0, j * GRID_W:(j + 1) * GRID_W, :] = jnp.concatenate(os, axis=1).astype(o_ref.dtype)


def _neighbourhood(q, k, v, bias):
    b, nh, s, hd = q.shape
    blk = NA_ROWS_PER_STEP * GRID_W
    nb = s // blk
    n_rows = s // GRID_W
    halo = NA_HALO_ROWS * GRID_W
    ratio = blk // halo
    n_halo = s // halo
    cur = pl.BlockSpec((1, nh, blk, hd), lambda bi, i: (bi, 0, i, 0))
    prev = pl.BlockSpec((1, nh, halo, hd), lambda bi, i: (bi, 0, jnp.maximum(i * ratio - 1, 0), 0))
    nxt = pl.BlockSpec((1, nh, halo, hd),
                       lambda bi, i: (bi, 0, jnp.minimum((i + 1) * ratio, n_halo - 1), 0))
    return pl.pallas_call(
        functools.partial(_na_kernel, n_rows=n_rows),
        grid=(b, nb),
        in_specs=[cur, prev, cur, nxt, prev, cur, nxt, _const_spec(bias.shape)],
        out_specs=pl.BlockSpec((1, blk, nh * hd), lambda bi, i: (bi, i, 0)),
        out_shape=jax.ShapeDtypeStruct((b, s, nh * hd), BF16),
        scratch_shapes=[pltpu.VMEM((nh, blk + 2 * halo, hd), BF16)] * 2,
        compiler_params=pltpu.CompilerParams(
            dimension_semantics=("parallel", "parallel"), vmem_limit_bytes=VMEM_LIMIT),
        name="neighbourhood_attention",
    )(q, k, k, k, v, v, v, bias)


def _na_bias_table(rpb):
    c = np.arange(GRID_W)[:, None]
    kc = np.arange(GRID_W)[None, :]
    cs = np.clip(c - NA_WIN_COLS // 2, 0, GRID_W - NA_WIN_COLS)
    valid = (kc >= cs) & (kc < cs + NA_WIN_COLS)
    col_off = kc - c + (NA_WIN_COLS - 1)
    n_off = 2 * NA_WIN_COLS - 1
    select = (valid[:, :, None] & (col_off[:, :, None] == np.arange(n_off))).astype(np.float32)
    x = jnp.einsum("hro,cko->hrck", rpb.astype(F32), jnp.asarray(select),
                   precision=lax.Precision.HIGHEST)
    x = jnp.where(valid[None, None], x, MASK_VALUE)
    t = jnp.stack([x[:, NA_WIN_ROWS - 1 - d:2 * NA_WIN_ROWS - 1 - d] for d in range(NA_WIN_ROWS)],
                  axis=1)
    t = t.transpose(0, 1, 3, 2, 4)
    return t.reshape(rpb.shape[0], NA_WIN_ROWS, GRID_W, NA_WIN_ROWS * GRID_W)


def _sw_kernel(sink_ref, q_ref, kp_ref, kc_ref, kn_ref, vp_ref, vc_ref, vn_ref, o_ref, *, seq):
    i = pl.program_id(1)
    tq = q_ref.shape[2]
    span = tq + 2 * SW_WINDOW
    t0 = i * tq
    row = lax.broadcasted_iota(jnp.int32, (tq, span), 0)
    col = lax.broadcasted_iota(jnp.int32, (tq, span), 1)
    kpos = col + (t0 - SW_WINDOW)
    rel = col - row
    valid = (rel >= 0) & (rel <= 2 * SW_WINDOW) & (kpos >= 0) & (kpos < seq)
    nt = (((1,), (1,)), ((), ()))
    rep = GROUP_HEADS // GROUP_KV_HEADS
    heads = range(GROUP_HEADS)
    ss = [jnp.where(valid, jnp.concatenate(
        [lax.dot_general(q_ref[0, hd], kr[0, hd // rep], nt, preferred_element_type=F32)
         for kr in (kp_ref, kc_ref, kn_ref)], axis=1), MASK_VALUE) for hd in heads]
    ms = [jnp.maximum(jnp.max(ss[hd], axis=-1, keepdims=True), sink_ref[hd]) for hd in heads]
    ps = [jnp.exp(ss[hd] - ms[hd]) for hd in heads]
    denoms = [jnp.sum(ps[hd], axis=-1, keepdims=True) + jnp.exp(sink_ref[hd] - ms[hd])
              for hd in heads]
    os = []
    for hd in heads:
        g = hd // rep
        pb = ps[hd].astype(BF16)
        o = (_bdot(pb[:, :SW_WINDOW], vp_ref[0, g])
             + _bdot(pb[:, SW_WINDOW:SW_WINDOW + tq], vc_ref[0, g])
             + _bdot(pb[:, SW_WINDOW + tq:], vn_ref[0, g]))
        os.append(o / denoms[hd])
    o_ref[0] = jnp.concatenate(os, axis=1).astype(o_ref.dtype)


def _sliding_window(q, k, v, sink, tq):
    b, nh, s, hd = q.shape
    nkv = k.shape[1]
    nb = s // tq
    r = tq // SW_WINDOW
    n_small = s // SW_WINDOW
    cur = pl.BlockSpec((1, nkv, tq, hd), lambda bi, i: (bi, 0, i, 0))
    prev = pl.BlockSpec((1, nkv, SW_WINDOW, hd), lambda bi, i: (bi, 0, jnp.maximum(i * r - 1, 0), 0))
    nxt = pl.BlockSpec((1, nkv, SW_WINDOW, hd),
                       lambda bi, i: (bi, 0, jnp.minimum((i + 1) * r, n_small - 1), 0))
    return pl.pallas_call(
        functools.partial(_sw_kernel, seq=s),
        grid=(b, nb),
        in_specs=[pl.BlockSpec(memory_space=pltpu.SMEM),
                  pl.BlockSpec((1, nh, tq, hd), lambda bi, i: (bi, 0, i, 0)),
                  prev, cur, nxt, prev, cur, nxt],
        out_specs=pl.BlockSpec((1, tq, nh * hd), lambda bi, i: (bi, i, 0)),
        out_shape=jax.ShapeDtypeStruct((b, s, nh * hd), BF16),
        compiler_params=pltpu.CompilerParams(
            dimension_semantics=("parallel", "parallel"), vmem_limit_bytes=VMEM_LIMIT),
        name="sliding_window_attention",
    )(sink, q, k, k, k, v, v, v)


def _out_kernel(x_ref, oa_ref, obT_ref, ocT_ref, od_ref, w_ref, g_ref, o_ref):
    mixed_in = jnp.concatenate(
        [oa_ref[0], obT_ref[0].astype(F32).T.astype(BF16), ocT_ref[0].astype(F32).T.astype(BF16),
         od_ref[0]], axis=1)
    mixed = _bdot(mixed_in, w_ref[...])
    o_ref[0] = x_ref[0] + _rms(mixed, g_ref[...])


def _out_projection(x, o_a, o_bT, o_cT, o_d, w_out, gain, tm):
    b, s, _ = x.shape
    tok = lambda w: pl.BlockSpec((1, tm, w), lambda bi, i: (bi, i, 0))
    feat = pl.BlockSpec((1, GROUP_WIDTH, tm), lambda bi, i: (bi, 0, i))
    return pl.pallas_call(
        _out_kernel,
        grid=(b, s // tm),
        in_specs=[tok(D_MODEL), tok(GROUP_WIDTH), feat, feat, tok(GROUP_WIDTH),
                  _const_spec((D_MODEL, D_MODEL)), _const_spec((1, D_MODEL))],
        out_specs=tok(D_MODEL),
        out_shape=jax.ShapeDtypeStruct(x.shape, F32),
        compiler_params=pltpu.CompilerParams(
            dimension_semantics=("parallel", "parallel"), vmem_limit_bytes=VMEM_LIMIT),
        name="out_projection",
    )(x, o_a, o_bT, o_cT, o_d, w_out, gain)


FFN_HALO = 8


def _ffn_kernel(x_ref, xp_ref, xn_ref, g_ref, wup_ref, cw_ref, cb_ref, wd_ref, pg_ref, o_ref,
                h_scr, acc_scr, u_even, u_odd, *, fc):
    n_chunks = D_FF // fc

    def cols(ref, c, base):
        return ref[:, pl.ds(pl.multiple_of(base + c * fc, LANES), fc)]

    i = pl.program_id(1)
    n_tiles = pl.num_programs(1)
    tm = x_ref.shape[1]
    ext = tm + 2 * FFN_HALO
    g = g_ref[...]
    hp = _rms(xp_ref[0], g) * (i > 0).astype(F32)
    hn = _rms(xn_ref[0], g) * (i < n_tiles - 1).astype(F32)
    h_scr[...] = jnp.concatenate([hp, _rms(x_ref[0], g), hn], axis=0).astype(BF16)
    acc_scr[...] = jnp.zeros_like(acc_scr)

    def produce(u_ref, c):
        hh = h_scr[...]
        u_ref[0] = _bdot(hh, cols(wup_ref, c, 0))
        u_ref[1] = _bdot(hh, cols(wup_ref, c, D_FF))

    def conv(u_ref, cw, cb):
        lo = FFN_HALO - 1
        return (cb + u_ref[lo:lo + tm] * cw[0:1] + u_ref[lo + 1:lo + 1 + tm] * cw[1:2]
                + u_ref[lo + 2:lo + 2 + tm] * cw[2:3])

    def consume(u_ref, c):
        gate = conv(u_ref.at[0], cols(cw_ref, c, 0), cols(cb_ref, c, 0))
        val = conv(u_ref.at[1], cols(cw_ref, c, D_FF), cols(cb_ref, c, D_FF))
        act = _gelu_tanh(gate) * val
        acc_scr[...] += _bdot(act.astype(BF16), wd_ref[pl.ds(pl.multiple_of(c * fc, fc), fc), :])

    def pair(jj):
        produce(u_odd, 2 * jj + 1)
        consume(u_even, 2 * jj)
        produce(u_even, 2 * jj + 2)
        consume(u_odd, 2 * jj + 1)

    def body(t, carry):
        for u in range(FFN_PAIRS_PER_TRIP):
            pair(t * FFN_PAIRS_PER_TRIP + u)
        return carry

    n_pairs = (n_chunks - 1) // 2
    tail = n_chunks - 2 * n_pairs
    n_trips = n_pairs // FFN_PAIRS_PER_TRIP
    produce(u_even, 0)
    lax.fori_loop(0, n_trips, body, 0)
    for jj in range(n_trips * FFN_PAIRS_PER_TRIP, n_pairs):
        pair(jj)
    if tail == 2:
        produce(u_odd, n_chunks - 1)
    consume(u_even, 2 * n_pairs)
    if tail == 2:
        consume(u_odd, n_chunks - 1)
    o_ref[0] = x_ref[0] + _rms(acc_scr[...], pg_ref[...])


def _layer_spec(stacked, l):
    shape = stacked.shape[1:]
    return pl.BlockSpec((None,) + shape, lambda *_: (l,) + (0,) * len(shape),
                        pipeline_mode=pl.Buffered(1))


def _ffn(x, lw, mlp, l, tm):
    b, s, _ = x.shape
    fc = FF_CHUNK
    assert D_FF % fc == 0 and fc % LANES == 0
    r = tm // FFN_HALO
    n_halo = s // FFN_HALO
    tile = pl.BlockSpec((1, tm, D_MODEL), lambda bi, i: (bi, i, 0))
    prev = pl.BlockSpec((1, FFN_HALO, D_MODEL), lambda bi, i: (bi, jnp.maximum(i * r - 1, 0), 0))
    nxt = pl.BlockSpec((1, FFN_HALO, D_MODEL),
                       lambda bi, i: (bi, jnp.minimum((i + 1) * r, n_halo - 1), 0))
    return pl.pallas_call(
        functools.partial(_ffn_kernel, fc=fc),
        grid=(b, s // tm),
        in_specs=[tile, prev, nxt, _const_spec((1, D_MODEL)),
                  _layer_spec(mlp["w_up"], l), _layer_spec(mlp["conv_w"], l),
                  _layer_spec(mlp["conv_b"], l), _layer_spec(mlp["w_down"], l),
                  _const_spec((1, D_MODEL))],
        out_specs=tile,
        out_shape=jax.ShapeDtypeStruct(x.shape, F32),
        scratch_shapes=[pltpu.VMEM((tm + 2 * FFN_HALO, D_MODEL), BF16),
                        pltpu.VMEM((tm, D_MODEL), F32),
                        pltpu.VMEM((2, tm + 2 * FFN_HALO, fc), F32),
                        pltpu.VMEM((2, tm + 2 * FFN_HALO, fc), F32)],
        compiler_params=pltpu.CompilerParams(
            dimension_semantics=("parallel", "parallel"), vmem_limit_bytes=VMEM_LIMIT),
        name="conv_mlp",
    )(x, x, x, lw["ffn_pre_gain"], mlp["w_up"], mlp["conv_w"], mlp["conv_b"], mlp["w_down"],
      lw["ffn_post_gain"])


def _rope_tables(s):
    t = jnp.arange(s)

    def angles(pos, dim):
        inv = ROPE_THETA ** (-jnp.arange(0, dim, 2, dtype=F32) / dim)
        return pos.astype(F32)[:, None] * inv[None, :]

    def lane_table(ang, lead, trail, reps):
        half = ang.shape[1]
        cos, sin, zero = jnp.cos(ang), jnp.sin(ang), jnp.zeros_like(ang)
        one = lambda n: jnp.ones((s, n), F32)
        nul = lambda n: jnp.zeros((s, n), F32)
        c = jnp.concatenate([one(lead)] + [cos, cos] * reps + [one(trail)], axis=1)
        lo = jnp.concatenate([nul(lead)] + [-sin, zero] * reps + [nul(trail)], axis=1)
        hi = jnp.concatenate([nul(lead)] + [zero, sin] * reps + [nul(trail)], axis=1)
        assert c.shape[1] == LANES and 2 * half * reps + lead + trail == LANES
        return jnp.stack([c, lo, hi])

    ang_row = angles(t // GRID_W, HEAD_DIM // 2)
    ang_col = angles(t % GRID_W, HEAD_DIM // 2)
    ang_mla = angles(t, MLA_ROPE)
    return {
        "mla": lane_table(ang_mla, MLA_NOPE, LANES - MLA_NOPE - MLA_ROPE, 1),
        "full": lane_table(angles(t, HEAD_DIM), 0, 0, LANES // HEAD_DIM),
        "axial": jnp.concatenate([jnp.cos(ang_row), jnp.sin(ang_row),
                                  jnp.cos(ang_col), jnp.sin(ang_col),
                                  jnp.cos(ang_mla), jnp.sin(ang_mla)], axis=1).T,
    }


def _layer_weights(l, mix_pre_gain, w_in, na_rpb, mla_q_gain, mla_w_uq, mla_kv_gain, mla_w_ukv,
                   ax_q_gain, ax_k_gain, sw_sink, w_out, mix_post_gain, ffn_pre_gain, w_up,
                   conv_w, conv_b, w_down, ffn_post_gain):
    gw, kvw = GROUP_WIDTH, GROUP_KV_HEADS * HEAD_DIM
    sizes = (gw, gw, gw, MLA_Q_RANK, MLA_KV_RANK, MLA_ROPE, gw, kvw, kvw, gw, kvw, kvw)
    bounds = np.cumsum((0,) + sizes)
    (a_q, a_k, a_v, b_cq, b_ckv, b_kr, c_q, c_k, c_v, d_q, d_k, d_v) = [
        w_in[l][:, bounds[j]:bounds[j + 1]] for j in range(len(sizes))]
    scale = HEAD_DIM ** -0.5
    zeros = lambda n: jnp.zeros((D_MODEL, n), F32)
    kr_block = jnp.concatenate([zeros(MLA_NOPE), b_kr, zeros(LANES - MLA_NOPE - MLA_ROPE)], axis=1)
    w_in_r = jnp.concatenate([a_q * scale, a_k, a_v, b_cq, b_ckv, kr_block, c_q, c_k, c_v,
                              d_q * scale, d_k, d_v], axis=1)
    assert w_in_r.shape[1] == _IN_COLS_PADDED

    uq = mla_w_uq[l].reshape(MLA_Q_RANK, GROUP_HEADS, MLA_NOPE + MLA_ROPE)
    uq = jnp.concatenate([uq[:, :, :MLA_NOPE].reshape(MLA_Q_RANK, -1),
                          uq[:, :, MLA_NOPE:].reshape(MLA_Q_RANK, -1)], axis=1)
    ukv = mla_w_ukv[l].reshape(MLA_KV_RANK, GROUP_HEADS, MLA_NOPE + HEAD_DIM)
    uk = jnp.pad(ukv[:, :, :MLA_NOPE], ((0, 0), (0, 0), (0, LANES - MLA_NOPE)))
    uv = ukv[:, :, MLA_NOPE:]

    row = lambda v: v[None, :].astype(F32)
    return {
        "pre_gain": row(mix_pre_gain[l]),
        "w_in": w_in_r.astype(BF16),
        "q_gain": row(mla_q_gain[l]),
        "w_uq": uq.astype(BF16),
        "kv_gain": row(mla_kv_gain[l]),
        "w_uk": uk.reshape(MLA_KV_RANK, GROUP_HEADS * LANES).astype(BF16),
        "w_uv": uv.reshape(MLA_KV_RANK, GROUP_WIDTH).astype(BF16),
        "cq_gain": (ax_q_gain[l] * (scale * LOG2E))[:, None].astype(F32),
        "ck_gain": ax_k_gain[l][:, None].astype(F32),
        "na_bias": _na_bias_table(na_rpb[l]),
        "sink": sw_sink[l].astype(F32),
        "w_out": w_out[l].astype(BF16),
        "post_gain": row(mix_post_gain[l]),
        "ffn_pre_gain": row(ffn_pre_gain[l]),
        "ffn_post_gain": row(ffn_post_gain[l]),
    }


def kernel(x, mix_pre_gain, w_in, na_rpb, mla_q_gain, mla_w_uq, mla_kv_gain, mla_w_ukv, ax_q_gain,
           ax_k_gain, sw_sink, w_out, mix_post_gain, ffn_pre_gain, w_up, conv_w, conv_b, w_down,
           ffn_post_gain):
    b, s, d = x.shape
    tiles = (TOKEN_TILE, FFN_TILE, FLASH_TQ * FLASH_STREAMS * FLASH_UNITS, 2 * FLASH_TK,
             NA_ROWS_PER_STEP * GRID_W, SW_TQ)
    assert d == D_MODEL and all(s % t == 0 for t in tiles)
    assert s // GRID_W >= NA_WIN_ROWS
    params = (mix_pre_gain, w_in, na_rpb, mla_q_gain, mla_w_uq, mla_kv_gain, mla_w_ukv, ax_q_gain,
              ax_k_gain, sw_sink, w_out, mix_post_gain, ffn_pre_gain, w_up, conv_w, conv_b, w_down,
              ffn_post_gain)
    tabs = _rope_tables(s)
    mlp = {"w_up": w_up.astype(BF16), "w_down": w_down.astype(BF16),
           "conv_w": conv_w.astype(F32), "conv_b": conv_b.astype(F32)[:, None, :]}
    for l in range(w_in.shape[0]):
        lw = _layer_weights(l, *params)
        (a_q, a_k, a_v, b_qT, b_k, b_vT, c_qT, c_k, c_vT, d_q, d_k, d_v) = _projection(
            x, lw, tabs, TOKEN_TILE)
        o_a = _neighbourhood(a_q, a_k, a_v, lw["na_bias"])
        o_bT = _flash(b_qT, b_k, b_vT, n_heads=GROUP_HEADS, n_kv=GROUP_HEADS, dk=LANES,
                      k_head_major=False, tq=FLASH_TQ * FLASH_STREAMS * FLASH_UNITS,
                      n_streams=FLASH_STREAMS)
        o_cT = _flash(c_qT, c_k, c_vT, n_heads=GROUP_HEADS, n_kv=GROUP_KV_HEADS, dk=HEAD_DIM,
                      k_head_major=True, tq=FLASH_TQ * FLASH_STREAMS * FLASH_UNITS,
                      n_streams=FLASH_STREAMS)
        o_d = _sliding_window(d_q, d_k, d_v, lw["sink"], SW_TQ)
        x = _out_projection(x, o_a, o_bT, o_cT, o_d, lw["w_out"], lw["post_gain"], TOKEN_TILE)
        x = _ffn(x, lw, mlp, l, FFN_TILE)
    return x
```

```python
import functools
import math

import numpy as np
import jax
import jax.numpy as jnp
from jax import lax
from jax.experimental import pallas as pl
from jax.experimental.pallas import tpu as pltpu

F32 = jnp.float32
BF16 = jnp.bfloat16

D_MODEL = 1024
GRID_W = 64
HEAD_DIM = 64
GROUP_HEADS = 4
GROUP_KV_HEADS = 2
GROUP_WIDTH = GROUP_HEADS * HEAD_DIM
ROPE_THETA = 10000.0
NORM_EPS = 1e-6
MASK_VALUE = -1e30
NA_WIN_ROWS = 8
NA_WIN_COLS = 16
MLA_Q_RANK = 256
MLA_KV_RANK = 128
MLA_NOPE = 64
MLA_ROPE = 32
SW_WINDOW = 128
D_FF = 2816
LOG2E = math.log2(math.e)

LANES = 128
VMEM_LIMIT = 56 * 1024 * 1024

BF16_SUBLANES = 16
TOKEN_TILE = 1024
FLASH_TQ = 512
FLASH_TK = 256
FLASH_STREAMS = 2
FLASH_UNITS = 2
FLASH_PAIRS_PER_TRIP = 5
V_EXT = HEAD_DIM + BF16_SUBLANES
NA_ROWS_PER_STEP = 8
NA_HALO_ROWS = NA_WIN_ROWS // 2
NA_ROWS_INTERLEAVED = 8
SW_TQ = 256
FFN_TILE = 1024
FFN_PAIRS_PER_TRIP = 1
FF_CHUNK = 256


def _rms(x, gain):
    return x * lax.rsqrt(jnp.mean(x * x, axis=-1, keepdims=True) + NORM_EPS) * gain


def _rope_lanes(x, tab_ref, half):
    w = x.shape[-1]
    return (x * tab_ref[0] + pltpu.roll(x, w - half, 1) * tab_ref[1]
            + pltpu.roll(x, half, 1) * tab_ref[2])


_GELU_C = math.sqrt(2.0 / math.pi)


def _gelu_tanh(x):
    k1 = -2.0 * _GELU_C * LOG2E
    k3 = k1 * 0.044715
    return x / (1.0 + jnp.exp2(x * (k1 + k3 * (x * x))))


def _bdot(a, b):
    return jnp.dot(a, b, preferred_element_type=F32)


_C_AQ, _C_AK, _C_AV = 0, 256, 512
_C_BCQ, _C_BCKV, _C_BKR = 768, 1024, 1152
_C_CQ, _C_CK, _C_CV = 1280, 1536, 1664
_C_DQ, _C_DK, _C_DV = 1792, 2048, 2176
_IN_COLS_PADDED = 2304
_PAIR = 2 * LANES


def _proj_kernel(x_ref, g_ref, win_ref, qg_ref, wuq_ref, kvg_ref, wuk_ref, wuv_ref,
                 cqg_ref, ckg_ref, tabb_ref, tabd_ref, tabc_ref,
                 aq_ref, ak_ref, av_ref, bqT_ref, bk_ref, bvT_ref,
                 cqT_ref, ck_ref, cvT_ref, dq_ref, dk_ref, dv_ref):
    h = _rms(x_ref[0], g_ref[...]).astype(BF16)

    def proj(c0, width):
        return _bdot(h, win_ref[:, c0:c0 + width])

    def store_heads(ref, z, n_heads):
        for hd in range(n_heads):
            ref[0, hd] = z[:, hd * HEAD_DIM:(hd + 1) * HEAD_DIM].astype(BF16)

    def store_vT_ext(ref, vT, n_heads):
        tk = ref.shape[3]
        pad = V_EXT - HEAD_DIM
        ones_row = (lax.broadcasted_iota(jnp.int32, (pad, tk), 0) == 0).astype(F32).astype(BF16)
        for t in range(ref.shape[1]):
            for hd in range(n_heads):
                ref[0, t, hd * V_EXT:hd * V_EXT + HEAD_DIM, :] = (
                    vT[hd * HEAD_DIM:(hd + 1) * HEAD_DIM, t * tk:(t + 1) * tk].astype(BF16))
                ref[0, t, hd * V_EXT + HEAD_DIM:(hd + 1) * V_EXT, :] = ones_row

    store_heads(aq_ref, proj(_C_AQ, GROUP_WIDTH), GROUP_HEADS)
    store_heads(ak_ref, proj(_C_AK, GROUP_WIDTH), GROUP_HEADS)
    store_heads(av_ref, proj(_C_AV, GROUP_WIDTH), GROUP_HEADS)

    cq = _rms(proj(_C_BCQ, MLA_Q_RANK), qg_ref[...]).astype(BF16)
    qb = _bdot(cq, wuq_ref[...]) * ((MLA_NOPE + MLA_ROPE) ** -0.5 * LOG2E)
    nope_w = GROUP_HEADS * MLA_NOPE
    q_nope_T = qb[:, :nope_w].T
    q_pe_T = qb[:, nope_w:].T
    half = MLA_ROPE // 2
    cos_m = tabc_ref[HEAD_DIM:HEAD_DIM + half]
    sin_m = tabc_ref[HEAD_DIM + half:HEAD_DIM + 2 * half]
    for hd in range(GROUP_HEADS):
        r0 = hd * LANES
        x1 = q_pe_T[hd * MLA_ROPE:hd * MLA_ROPE + half]
        x2 = q_pe_T[hd * MLA_ROPE + half:(hd + 1) * MLA_ROPE]
        bqT_ref[0, r0:r0 + MLA_NOPE, :] = q_nope_T[hd * MLA_NOPE:(hd + 1) * MLA_NOPE].astype(BF16)
        bqT_ref[0, r0 + MLA_NOPE:r0 + MLA_NOPE + half, :] = (x1 * cos_m - x2 * sin_m).astype(BF16)
        bqT_ref[0, r0 + MLA_NOPE + half:r0 + MLA_NOPE + MLA_ROPE, :] = (
            x2 * cos_m + x1 * sin_m).astype(BF16)
        bqT_ref[0, r0 + MLA_NOPE + MLA_ROPE:r0 + LANES, :] = jnp.zeros(
            (LANES - MLA_NOPE - MLA_ROPE, qb.shape[0]), BF16)
    assert (_C_BKR, _C_CV, _C_DV) == (_C_BCKV + LANES, _C_CK + LANES, _C_DK + LANES)
    b_pair = proj(_C_BCKV, _PAIR)
    kpe = _rope_lanes(b_pair[:, LANES:], tabb_ref, MLA_ROPE // 2)
    ckv = _rms(b_pair[:, :LANES], kvg_ref[...]).astype(BF16)
    kn = _bdot(ckv, wuk_ref[...])
    for hd in range(GROUP_HEADS):
        blk = slice(hd * LANES, (hd + 1) * LANES)
        bk_ref[0, :, blk] = (kn[:, blk] + kpe).astype(BF16)
    store_vT_ext(bvT_ref, _bdot(ckv, wuv_ref[...]).T, GROUP_HEADS)

    def norm_rope_T(blk, gain_col):
        ms = jnp.mean(blk * blk, axis=0, keepdims=True)
        blk = blk * lax.rsqrt(ms + NORM_EPS) * gain_col
        q = HEAD_DIM // 4
        cr, sr = tabc_ref[0:q], tabc_ref[q:2 * q]
        cc, sc = tabc_ref[2 * q:3 * q], tabc_ref[3 * q:4 * q]
        x1, x2, x3, x4 = blk[0:q], blk[q:2 * q], blk[2 * q:3 * q], blk[3 * q:4 * q]
        return jnp.concatenate([x1 * cr - x2 * sr, x2 * cr + x1 * sr,
                                x3 * cc - x4 * sc, x4 * cc + x3 * sc], axis=0)

    cqT = proj(_C_CQ, GROUP_WIDTH).T
    for hd in range(GROUP_HEADS):
        rows = slice(hd * HEAD_DIM, (hd + 1) * HEAD_DIM)
        cqT_ref[0, rows, :] = norm_rope_T(cqT[rows], cqg_ref[...]).astype(BF16)
    c_pair = proj(_C_CK, _PAIR)
    ckT = c_pair[:, :LANES].T
    ck = jnp.concatenate(
        [norm_rope_T(ckT[hd * HEAD_DIM:(hd + 1) * HEAD_DIM], ckg_ref[...])
         for hd in range(GROUP_KV_HEADS)], axis=0).T
    store_heads(ck_ref, ck, GROUP_KV_HEADS)
    store_vT_ext(cvT_ref, c_pair[:, LANES:].T, GROUP_KV_HEADS)

    dq = proj(_C_DQ, GROUP_WIDTH)
    dq = jnp.concatenate([_rope_lanes(dq[:, j * LANES:(j + 1) * LANES], tabd_ref, HEAD_DIM // 2)
                          for j in range(GROUP_WIDTH // LANES)], axis=1)
    store_heads(dq_ref, dq, GROUP_HEADS)
    d_pair = proj(_C_DK, _PAIR)
    dk = _rope_lanes(d_pair[:, :LANES], tabd_ref, HEAD_DIM // 2)
    store_heads(dk_ref, dk, GROUP_KV_HEADS)
    store_heads(dv_ref, d_pair[:, LANES:], GROUP_KV_HEADS)


def _const_spec(shape):
    n = len(shape)
    return pl.BlockSpec(shape, lambda *_: (0,) * n, pipeline_mode=pl.Buffered(1))


def _projection(x, lw, tabs, tm):
    b, s, _ = x.shape
    nt = s // tm
    nk = s // FLASH_TK
    assert tm % FLASH_TK == 0
    head_q = jax.ShapeDtypeStruct((b, GROUP_HEADS, s, HEAD_DIM), BF16)
    head_kv = jax.ShapeDtypeStruct((b, GROUP_KV_HEADS, s, HEAD_DIM), BF16)
    out_shape = (
        head_q, head_q, head_q,
        jax.ShapeDtypeStruct((b, GROUP_HEADS * LANES, s), BF16),
        jax.ShapeDtypeStruct((b, s, GROUP_HEADS * LANES), BF16),
        jax.ShapeDtypeStruct((b, nk, GROUP_HEADS * V_EXT, FLASH_TK), BF16),
        jax.ShapeDtypeStruct((b, GROUP_WIDTH, s), BF16),
        head_kv,
        jax.ShapeDtypeStruct((b, nk, GROUP_KV_HEADS * V_EXT, FLASH_TK), BF16),
        head_q, head_kv, head_kv,
    )
    hq_spec = pl.BlockSpec((1, GROUP_HEADS, tm, HEAD_DIM), lambda bi, i: (bi, 0, i, 0))
    hkv_spec = pl.BlockSpec((1, GROUP_KV_HEADS, tm, HEAD_DIM), lambda bi, i: (bi, 0, i, 0))
    out_specs = (
        hq_spec, hq_spec, hq_spec,
        pl.BlockSpec((1, GROUP_HEADS * LANES, tm), lambda bi, i: (bi, 0, i)),
        pl.BlockSpec((1, tm, GROUP_HEADS * LANES), lambda bi, i: (bi, i, 0)),
        pl.BlockSpec((1, tm // FLASH_TK, GROUP_HEADS * V_EXT, FLASH_TK),
                     lambda bi, i: (bi, i, 0, 0)),
        pl.BlockSpec((1, GROUP_WIDTH, tm), lambda bi, i: (bi, 0, i)),
        hkv_spec,
        pl.BlockSpec((1, tm // FLASH_TK, GROUP_KV_HEADS * V_EXT, FLASH_TK),
                     lambda bi, i: (bi, i, 0, 0)),
        hq_spec, hkv_spec, hkv_spec,
    )
    in_specs = [
        pl.BlockSpec((1, tm, D_MODEL), lambda bi, i: (bi, i, 0)),
        _const_spec((1, D_MODEL)),
        _const_spec((D_MODEL, _IN_COLS_PADDED)),
        _const_spec((1, MLA_Q_RANK)),
        _const_spec((MLA_Q_RANK, GROUP_HEADS * (MLA_NOPE + MLA_ROPE))),
        _const_spec((1, MLA_KV_RANK)),
        _const_spec((MLA_KV_RANK, GROUP_HEADS * LANES)),
        _const_spec((MLA_KV_RANK, GROUP_WIDTH)),
        _const_spec((HEAD_DIM, 1)),
        _const_spec((HEAD_DIM, 1)),
        pl.BlockSpec((3, tm, LANES), lambda bi, i: (0, i, 0)),
        pl.BlockSpec((3, tm, LANES), lambda bi, i: (0, i, 0)),
        pl.BlockSpec((HEAD_DIM + MLA_ROPE, tm), lambda bi, i: (0, i)),
    ]
    return pl.pallas_call(
        _proj_kernel,
        grid=(b, nt),
        in_specs=in_specs,
        out_specs=out_specs,
        out_shape=out_shape,
        compiler_params=pltpu.CompilerParams(
            dimension_semantics=("parallel", "parallel"), vmem_limit_bytes=VMEM_LIMIT),
        name="projection",
    )(x, lw["pre_gain"], lw["w_in"], lw["q_gain"], lw["w_uq"], lw["kv_gain"], lw["w_uk"],
      lw["w_uv"], lw["cq_gain"], lw["ck_gain"], tabs["mla"], tabs["full"], tabs["axial"])


def _flash_kernel(qT_ref, k_ref, vT_ref, oT_ref, s_even, s_odd, *, tk, n_chunks, n_streams):
    tq = qT_ref.shape[2] // (n_streams * FLASH_UNITS)
    dv_ext = vT_ref.shape[2]
    for unit in range(FLASH_UNITS):
        _flash_unit(qT_ref, k_ref, vT_ref, oT_ref, s_even, s_odd, unit * n_streams * tq, tq, dv_ext,
                    tk=tk, n_chunks=n_chunks, n_streams=n_streams)


def _flash_unit(qT_ref, k_ref, vT_ref, oT_ref, s_even, s_odd, q0, tq, dv_ext, *, tk, n_chunks,
                n_streams):
    def produce(s_ref, c):
        k = k_ref[0, pl.ds(pl.multiple_of(c * tk, tk), tk), :]
        maxes = []
        for st in range(n_streams):
            s = _bdot(k, qT_ref[0, :, q0 + st * tq:q0 + (st + 1) * tq])
            s_ref[st] = s
            maxes.append(jnp.max(s, axis=0, keepdims=True))
        return tuple(maxes)

    def consume(s_ref, chunk_max, c, carry):
        vT = vT_ref[0, c]
        out = []
        for st, (m, acc) in enumerate(carry):
            m_new = jnp.maximum(m, chunk_max[st])
            p = jnp.exp2(s_ref[st] - m_new).astype(BF16)
            acc = jnp.exp2(m - m_new) * acc + _bdot(vT, p)
            out.append((m_new, acc))
        return tuple(out)

    def pair(jj, state):
        carry, max_even = state
        max_odd = produce(s_odd, 2 * jj + 1)
        carry = consume(s_even, max_even, 2 * jj, carry)
        max_even = produce(s_even, 2 * jj + 2)
        return consume(s_odd, max_odd, 2 * jj + 1, carry), max_even

    def body(t, state):
        for u in range(FLASH_PAIRS_PER_TRIP):
            state = pair(t * FLASH_PAIRS_PER_TRIP + u, state)
        return state

    carry = tuple((jnp.full((1, tq), -jnp.inf, F32), jnp.zeros((dv_ext, tq), F32))
                  for _ in range(n_streams))
    max_even = produce(s_even, 0)
    n_pairs = n_chunks // 2 - 1
    n_trips = n_pairs // FLASH_PAIRS_PER_TRIP
    state = lax.fori_loop(0, n_trips, body, (carry, max_even))
    for jj in range(n_trips * FLASH_PAIRS_PER_TRIP, n_pairs):
        state = pair(jj, state)
    carry, max_even = state
    max_odd = produce(s_odd, n_chunks - 1)
    carry = consume(s_even, max_even, n_chunks - 2, carry)
    carry = consume(s_odd, max_odd, n_chunks - 1, carry)
    for st, (_, acc) in enumerate(carry):
        oT_ref[0, :, q0 + st * tq:q0 + (st + 1) * tq] = (
            acc[:HEAD_DIM] / acc[HEAD_DIM:HEAD_DIM + 1]).astype(oT_ref.dtype)


def _flash(qT, k, vT, *, n_heads, n_kv, dk, k_head_major, tq, n_streams):
    b, _, s = qT.shape
    n_chunks, tk = vT.shape[1], vT.shape[3]
    assert n_chunks % 2 == 0 and s % tq == 0
    dv = HEAD_DIM
    rep = n_heads // n_kv
    if k_head_major:
        k_spec = pl.BlockSpec((None, 1, s, dk), lambda bi, h, i: (bi, h // rep, 0, 0))
    else:
        k_spec = pl.BlockSpec((1, s, dk), lambda bi, h, i: (bi, 0, h // rep))
    return pl.pallas_call(
        functools.partial(_flash_kernel, tk=tk, n_chunks=n_chunks, n_streams=n_streams),
        grid=(b, n_heads, s // tq),
        in_specs=[
            pl.BlockSpec((1, dk, tq), lambda bi, h, i: (bi, h, i)),
            k_spec,
            pl.BlockSpec((1, n_chunks, V_EXT, tk), lambda bi, h, i: (bi, 0, h // rep, 0)),
        ],
        out_specs=pl.BlockSpec((1, dv, tq), lambda bi, h, i: (bi, h, i)),
        out_shape=jax.ShapeDtypeStruct((b, n_heads * dv, s), BF16),
        scratch_shapes=[pltpu.VMEM((n_streams, tk, tq // (n_streams * FLASH_UNITS)), F32)] * 2,
        compiler_params=pltpu.CompilerParams(
            dimension_semantics=("parallel", "parallel", "parallel"),
            vmem_limit_bytes=VMEM_LIMIT),
        name="dense_attention",
    )(qT, k, vT)


def _na_kernel(q_ref, kp_ref, kc_ref, kn_ref, vp_ref, vc_ref, vn_ref, bias_ref, o_ref,
               k_win, v_win, *, n_rows):
    i = pl.program_id(1)
    blk = NA_ROWS_PER_STEP * GRID_W
    halo = NA_HALO_ROWS * GRID_W
    band = NA_WIN_ROWS * GRID_W
    for r0, r1, kr, vr in ((0, halo, kp_ref, vp_ref), (halo, halo + blk, kc_ref, vc_ref),
                           (halo + blk, 2 * halo + blk, kn_ref, vn_ref)):
        k_win[:, r0:r1, :] = kr[0]
        v_win[:, r0:r1, :] = vr[0]
    win_row0 = i * NA_ROWS_PER_STEP - NA_HALO_ROWS
    heads = range(GROUP_HEADS)
    for j0 in range(0, NA_ROWS_PER_STEP, NA_ROWS_INTERLEAVED):
        rows = range(j0, j0 + NA_ROWS_INTERLEAVED)
        offs, ds = {}, {}
        for j in rows:
            r = i * NA_ROWS_PER_STEP + j
            rs = jnp.clip(r - NA_WIN_ROWS // 2, 0, n_rows - NA_WIN_ROWS)
            offs[j] = pl.multiple_of((rs - win_row0) * GRID_W, GRID_W)
            ds[j] = r - rs
        ss = {(j, hd): lax.dot_general(q_ref[0, hd, j * GRID_W:(j + 1) * GRID_W, :],
                                       k_win[hd, pl.ds(offs[j], band), :],
                                       (((1,), (1,)), ((), ())), preferred_element_type=F32)
              + bias_ref[hd, ds[j]] for j in rows for hd in heads}
        ps = {key: jnp.exp(s - jnp.max(s, axis=-1, keepdims=True)) for key, s in ss.items()}
        for j in rows:
            os = [_bdot(ps[j, hd].astype(BF16), v_win[hd, pl.ds(offs[j], band), :])
                  / jnp.sum(ps[j, hd], axis=-1, keepdims=True) for hd in heads]
            o_ref[0, j * GRID_W:(j + 1) * GRID_W, :] = jnp.concatenate(os, axis=1).astype(o_ref.dtype)


def _neighbourhood(q, k, v, bias):
    b, nh, s, hd = q.shape
    blk = NA_ROWS_PER_STEP * GRID_W
    nb = s // blk
    n_rows = s // GRID_W
    halo = NA_HALO_ROWS * GRID_W
    ratio = blk // halo
    n_halo = s // halo
    cur = pl.BlockSpec((1, nh, blk, hd), lambda bi, i: (bi, 0, i, 0))
    prev = pl.BlockSpec((1, nh, halo, hd), lambda bi, i: (bi, 0, jnp.maximum(i * ratio - 1, 0), 0))
    nxt = pl.BlockSpec((1, nh, halo, hd),
                       lambda bi, i: (bi, 0, jnp.minimum((i + 1) * ratio, n_halo - 1), 0))
    return pl.pallas_call(
        functools.partial(_na_kernel, n_rows=n_rows),
        grid=(b, nb),
        in_specs=[cur, prev, cur, nxt, prev, cur, nxt, _const_spec(bias.shape)],
        out_specs=pl.BlockSpec((1, blk, nh * hd), lambda bi, i: (bi, i, 0)),
        out_shape=jax.ShapeDtypeStruct((b, s, nh * hd), BF16),
        scratch_shapes=[pltpu.VMEM((nh, blk + 2 * halo, hd), BF16)] * 2,
        compiler_params=pltpu.CompilerParams(
            dimension_semantics=("parallel", "parallel"), vmem_limit_bytes=VMEM_LIMIT),
        name="neighbourhood_attention",
    )(q, k, k, k, v, v, v, bias)


def _na_bias_table(rpb):
    c = np.arange(GRID_W)[:, None]
    kc = np.arange(GRID_W)[None, :]
    cs = np.clip(c - NA_WIN_COLS // 2, 0, GRID_W - NA_WIN_COLS)
    valid = (kc >= cs) & (kc < cs + NA_WIN_COLS)
    col_off = kc - c + (NA_WIN_COLS - 1)
    n_off = 2 * NA_WIN_COLS - 1
    select = (valid[:, :, None] & (col_off[:, :, None] == np.arange(n_off))).astype(np.float32)
    x = jnp.einsum("hro,cko->hrck", rpb.astype(F32), jnp.asarray(select),
                   precision=lax.Precision.HIGHEST)
    x = jnp.where(valid[None, None], x, MASK_VALUE)
    t = jnp.stack([x[:, NA_WIN_ROWS - 1 - d:2 * NA_WIN_ROWS - 1 - d] for d in range(NA_WIN_ROWS)],
                  axis=1)
    t = t.transpose(0, 1, 3, 2, 4)
    return t.reshape(rpb.shape[0], NA_WIN_ROWS, GRID_W, NA_WIN_ROWS * GRID_W)


def _sw_kernel(sink_ref, q_ref, kp_ref, kc_ref, kn_ref, vp_ref, vc_ref, vn_ref, o_ref, *, seq):
    i = pl.program_id(1)
    tq = q_ref.shape[2]
    span = tq + 2 * SW_WINDOW
    t0 = i * tq
    row = lax.broadcasted_iota(jnp.int32, (tq, span), 0)
    col = lax.broadcasted_iota(jnp.int32, (tq, span), 1)
    kpos = col + (t0 - SW_WINDOW)
    rel = col - row
    valid = (rel >= 0) & (rel <= 2 * SW_WINDOW) & (kpos >= 0) & (kpos < seq)
    nt = (((1,), (1,)), ((), ()))
    rep = GROUP_HEADS // GROUP_KV_HEADS
    heads = range(GROUP_HEADS)
    ss = [jnp.where(valid, jnp.concatenate(
        [lax.dot_general(q_ref[0, hd], kr[0, hd // rep], nt, preferred_element_type=F32)
         for kr in (kp_ref, kc_ref, kn_ref)], axis=1), MASK_VALUE) for hd in heads]
    ms = [jnp.maximum(jnp.max(ss[hd], axis=-1, keepdims=True), sink_ref[hd]) for hd in heads]
    ps = [jnp.exp(ss[hd] - ms[hd]) for hd in heads]
    denoms = [jnp.sum(ps[hd], axis=-1, keepdims=True) + jnp.exp(sink_ref[hd] - ms[hd])
              for hd in heads]
    os = []
    for hd in heads:
        g = hd // rep
        pb = ps[hd].astype(BF16)
        o = (_bdot(pb[:, :SW_WINDOW], vp_ref[0, g])
             + _bdot(pb[:, SW_WINDOW:SW_WINDOW + tq], vc_ref[0, g])
             + _bdot(pb[:, SW_WINDOW + tq:], vn_ref[0, g]))
        os.append(o / denoms[hd])
    o_ref[0] = jnp.concatenate(os, axis=1).astype(o_ref.dtype)


def _sliding_window(q, k, v, sink, tq):
    b, nh, s, hd = q.shape
    nkv = k.shape[1]
    nb = s // tq
    r = tq // SW_WINDOW
    n_small = s // SW_WINDOW
    cur = pl.BlockSpec((1, nkv, tq, hd), lambda bi, i: (bi, 0, i, 0))
    prev = pl.BlockSpec((1, nkv, SW_WINDOW, hd), lambda bi, i: (bi, 0, jnp.maximum(i * r - 1, 0), 0))
    nxt = pl.BlockSpec((1, nkv, SW_WINDOW, hd),
                       lambda bi, i: (bi, 0, jnp.minimum((i + 1) * r, n_small - 1), 0))
    return pl.pallas_call(
        functools.partial(_sw_kernel, seq=s),
        grid=(b, nb),
        in_specs=[pl.BlockSpec(memory_space=pltpu.SMEM),
                  pl.BlockSpec((1, nh, tq, hd), lambda bi, i: (bi, 0, i, 0)),
                  prev, cur, nxt, prev, cur, nxt],
        out_specs=pl.BlockSpec((1, tq, nh * hd), lambda bi, i: (bi, i, 0)),
        out_shape=jax.ShapeDtypeStruct((b, s, nh * hd), BF16),
        compiler_params=pltpu.CompilerParams(
            dimension_semantics=("parallel", "parallel"), vmem_limit_bytes=VMEM_LIMIT),
        name="sliding_window_attention",
    )(sink, q, k, k, k, v, v, v)


def _out_kernel(x_ref, oa_ref, obT_ref, ocT_ref, od_ref, w_ref, g_ref, o_ref):
    mixed_in = jnp.concatenate(
        [oa_ref[0], obT_ref[0].astype(F32).T.astype(BF16), ocT_ref[0].astype(F32).T.astype(BF16),
         od_ref[0]], axis=1)
    mixed = _bdot(mixed_in, w_ref[...])
    o_ref[0] = x_ref[0] + _rms(mixed, g_ref[...])


def _out_projection(x, o_a, o_bT, o_cT, o_d, w_out, gain, tm):
    b, s, _ = x.shape
    tok = lambda w: pl.BlockSpec((1, tm, w), lambda bi, i: (bi, i, 0))
    feat = pl.BlockSpec((1, GROUP_WIDTH, tm), lambda bi, i: (bi, 0, i))
    return pl.pallas_call(
        _out_kernel,
        grid=(b, s // tm),
        in_specs=[tok(D_MODEL), tok(GROUP_WIDTH), feat, feat, tok(GROUP_WIDTH),
                  _const_spec((D_MODEL, D_MODEL)), _const_spec((1, D_MODEL))],
        out_specs=tok(D_MODEL),
        out_shape=jax.ShapeDtypeStruct(x.shape, F32),
        compiler_params=pltpu.CompilerParams(
            dimension_semantics=("parallel", "parallel"), vmem_limit_bytes=VMEM_LIMIT),
        name="out_projection",
    )(x, o_a, o_bT, o_cT, o_d, w_out, gain)


FFN_HALO = 8


def _ffn_kernel(x_ref, xp_ref, xn_ref, g_ref, wup_ref, cw_ref, cb_ref, wd_ref, pg_ref, o_ref,
                h_scr, acc_scr, u_even, u_odd, *, fc):
    n_chunks = D_FF // fc

    def cols(ref, c, base):
        return ref[:, pl.ds(pl.multiple_of(base + c * fc, LANES), fc)]

    i = pl.program_id(1)
    n_tiles = pl.num_programs(1)
    tm = x_ref.shape[1]
    ext = tm + 2 * FFN_HALO
    g = g_ref[...]
    hp = _rms(xp_ref[0], g) * (i > 0).astype(F32)
    hn = _rms(xn_ref[0], g) * (i < n_tiles - 1).astype(F32)
    h_scr[...] = jnp.concatenate([hp, _rms(x_ref[0], g), hn], axis=0).astype(BF16)
    acc_scr[...] = jnp.zeros_like(acc_scr)

    def produce(u_ref, c):
        hh = h_scr[...]
        u_ref[0] = _bdot(hh, cols(wup_ref, c, 0))
        u_ref[1] = _bdot(hh, cols(wup_ref, c, D_FF))

    def conv(u_ref, cw, cb):
        lo = FFN_HALO - 1
        return (cb + u_ref[lo:lo + tm] * cw[0:1] + u_ref[lo + 1:lo + 1 + tm] * cw[1:2]
                + u_ref[lo + 2:lo + 2 + tm] * cw[2:3])

    def consume(u_ref, c):
        gate = conv(u_ref.at[0], cols(cw_ref, c, 0), cols(cb_ref, c, 0))
        val = conv(u_ref.at[1], cols(cw_ref, c, D_FF), cols(cb_ref, c, D_FF))
        act = _gelu_tanh(gate) * val
        acc_scr[...] += _bdot(act.astype(BF16), wd_ref[pl.ds(pl.multiple_of(c * fc, fc), fc), :])

    def pair(jj):
        produce(u_odd, 2 * jj + 1)
        consume(u_even, 2 * jj)
        produce(u_even, 2 * jj + 2)
        consume(u_odd, 2 * jj + 1)

    def body(t, carry):
        for u in range(FFN_PAIRS_PER_TRIP):
            pair(t * FFN_PAIRS_PER_TRIP + u)
        return carry

    n_pairs = (n_chunks - 1) // 2
    tail = n_chunks - 2 * n_pairs
    n_trips = n_pairs // FFN_PAIRS_PER_TRIP
    produce(u_even, 0)
    lax.fori_loop(0, n_trips, body, 0)
    for jj in range(n_trips * FFN_PAIRS_PER_TRIP, n_pairs):
        pair(jj)
    if tail == 2:
        produce(u_odd, n_chunks - 1)
    consume(u_even, 2 * n_pairs)
    if tail == 2:
        consume(u_odd, n_chunks - 1)
    o_ref[0] = x_ref[0] + _rms(acc_scr[...], pg_ref[...])


def _layer_spec(stacked, l):
    shape = stacked.shape[1:]
    return pl.BlockSpec((None,) + shape, lambda *_: (l,) + (0,) * len(shape),
                        pipeline_mode=pl.Buffered(1))


def _ffn(x, lw, mlp, l, tm):
    b, s, _ = x.shape
    fc = FF_CHUNK
    assert D_FF % fc == 0 and fc % LANES == 0
    r = tm // FFN_HALO
    n_halo = s // FFN_HALO
    tile = pl.BlockSpec((1, tm, D_MODEL), lambda bi, i: (bi, i, 0))
    prev = pl.BlockSpec((1, FFN_HALO, D_MODEL), lambda bi, i: (bi, jnp.maximum(i * r - 1, 0), 0))
    nxt = pl.BlockSpec((1, FFN_HALO, D_MODEL),
                       lambda bi, i: (bi, jnp.minimum((i + 1) * r, n_halo - 1), 0))
    return pl.pallas_call(
        functools.partial(_ffn_kernel, fc=fc),
        grid=(b, s // tm),
        in_specs=[tile, prev, nxt, _const_spec((1, D_MODEL)),
                  _layer_spec(mlp["w_up"], l), _layer_spec(mlp["conv_w"], l),
                  _layer_spec(mlp["conv_b"], l), _layer_spec(mlp["w_down"], l),
                  _const_spec((1, D_MODEL))],
        out_specs=tile,
        out_shape=jax.ShapeDtypeStruct(x.shape, F32),
        scratch_shapes=[pltpu.VMEM((tm + 2 * FFN_HALO, D_MODEL), BF16),
                        pltpu.VMEM((tm, D_MODEL), F32),
                        pltpu.VMEM((2, tm + 2 * FFN_HALO, fc), F32),
                        pltpu.VMEM((2, tm + 2 * FFN_HALO, fc), F32)],
        compiler_params=pltpu.CompilerParams(
            dimension_semantics=("parallel", "parallel"), vmem_limit_bytes=VMEM_LIMIT),
        name="conv_mlp",
    )(x, x, x, lw["ffn_pre_gain"], mlp["w_up"], mlp["conv_w"], mlp["conv_b"], mlp["w_down"],
      lw["ffn_post_gain"])


def _rope_tables(s):
    t = jnp.arange(s)

    def angles(pos, dim):
        inv = ROPE_THETA ** (-jnp.arange(0, dim, 2, dtype=F32) / dim)
        return pos.astype(F32)[:, None] * inv[None, :]

    def lane_table(ang, lead, trail, reps):
        half = ang.shape[1]
        cos, sin, zero = jnp.cos(ang), jnp.sin(ang), jnp.zeros_like(ang)
        one = lambda n: jnp.ones((s, n), F32)
        nul = lambda n: jnp.zeros((s, n), F32)
        c = jnp.concatenate([one(lead)] + [cos, cos] * reps + [one(trail)], axis=1)
        lo = jnp.concatenate([nul(lead)] + [-sin, zero] * reps + [nul(trail)], axis=1)
        hi = jnp.concatenate([nul(lead)] + [zero, sin] * reps + [nul(trail)], axis=1)
        assert c.shape[1] == LANES and 2 * half * reps + lead + trail == LANES
        return jnp.stack([c, lo, hi])

    ang_row = angles(t // GRID_W, HEAD_DIM // 2)
    ang_col = angles(t % GRID_W, HEAD_DIM // 2)
    ang_mla = angles(t, MLA_ROPE)
    return {
        "mla": lane_table(ang_mla, MLA_NOPE, LANES - MLA_NOPE - MLA_ROPE, 1),
        "full": lane_table(angles(t, HEAD_DIM), 0, 0, LANES // HEAD_DIM),
        "axial": jnp.concatenate([jnp.cos(ang_row), jnp.sin(ang_row),
                                  jnp.cos(ang_col), jnp.sin(ang_col),
                                  jnp.cos(ang_mla), jnp.sin(ang_mla)], axis=1).T,
    }


def _layer_weights(l, mix_pre_gain, w_in, na_rpb, mla_q_gain, mla_w_uq, mla_kv_gain, mla_w_ukv,
                   ax_q_gain, ax_k_gain, sw_sink, w_out, mix_post_gain, ffn_pre_gain, w_up,
                   conv_w, conv_b, w_down, ffn_post_gain):
    gw, kvw = GROUP_WIDTH, GROUP_KV_HEADS * HEAD_DIM
    sizes = (gw, gw, gw, MLA_Q_RANK, MLA_KV_RANK, MLA_ROPE, gw, kvw, kvw, gw, kvw, kvw)
    bounds = np.cumsum((0,) + sizes)
    (a_q, a_k, a_v, b_cq, b_ckv, b_kr, c_q, c_k, c_v, d_q, d_k, d_v) = [
        w_in[l][:, bounds[j]:bounds[j + 1]] for j in range(len(sizes))]
    scale = HEAD_DIM ** -0.5
    zeros = lambda n: jnp.zeros((D_MODEL, n), F32)
    kr_block = jnp.concatenate([zeros(MLA_NOPE), b_kr, zeros(LANES - MLA_NOPE - MLA_ROPE)], axis=1)
    w_in_r = jnp.concatenate([a_q * scale, a_k, a_v, b_cq, b_ckv, kr_block, c_q, c_k, c_v,
                              d_q * scale, d_k, d_v], axis=1)
    assert w_in_r.shape[1] == _IN_COLS_PADDED

    uq = mla_w_uq[l].reshape(MLA_Q_RANK, GROUP_HEADS, MLA_NOPE + MLA_ROPE)
    uq = jnp.concatenate([uq[:, :, :MLA_NOPE].reshape(MLA_Q_RANK, -1),
                          uq[:, :, MLA_NOPE:].reshape(MLA_Q_RANK, -1)], axis=1)
    ukv = mla_w_ukv[l].reshape(MLA_KV_RANK, GROUP_HEADS, MLA_NOPE + HEAD_DIM)
    uk = jnp.pad(ukv[:, :, :MLA_NOPE], ((0, 0), (0, 0), (0, LANES - MLA_NOPE)))
    uv = ukv[:, :, MLA_NOPE:]

    row = lambda v: v[None, :].astype(F32)
    return {
        "pre_gain": row(mix_pre_gain[l]),
        "w_in": w_in_r.astype(BF16),
        "q_gain": row(mla_q_gain[l]),
        "w_uq": uq.astype(BF16),
        "kv_gain": row(mla_kv_gain[l]),
        "w_uk": uk.reshape(MLA_KV_RANK, GROUP_HEADS * LANES).astype(BF16),
        "w_uv": uv.reshape(MLA_KV_RANK, GROUP_WIDTH).astype(BF16),
        "cq_gain": (ax_q_gain[l] * (scale * LOG2E))[:, None].astype(F32),
        "ck_gain": ax_k_gain[l][:, None].astype(F32),
        "na_bias": _na_bias_table(na_rpb[l]),
        "sink": sw_sink[l].astype(F32),
        "w_out": w_out[l].astype(BF16),
        "post_gain": row(mix_post_gain[l]),
        "ffn_pre_gain": row(ffn_pre_gain[l]),
        "ffn_post_gain": row(ffn_post_gain[l]),
    }


def kernel(x, mix_pre_gain, w_in, na_rpb, mla_q_gain, mla_w_uq, mla_kv_gain, mla_w_ukv, ax_q_gain,
           ax_k_gain, sw_sink, w_out, mix_post_gain, ffn_pre_gain, w_up, conv_w, conv_b, w_down,
           ffn_post_gain):
    b, s, d = x.shape
    tiles = (TOKEN_TILE, FFN_TILE, FLASH_TQ * FLASH_STREAMS * FLASH_UNITS, 2 * FLASH_TK,
             NA_ROWS_PER_STEP * GRID_W, SW_TQ)
    assert d == D_MODEL and all(s % t == 0 for t in tiles)
    assert s // GRID_W >= NA_WIN_ROWS
    params = (mix_pre_gain, w_in, na_rpb, mla_q_gain, mla_w_uq, mla_kv_gain, mla_w_ukv, ax_q_gain,
              ax_k_gain, sw_sink, w_out, mix_post_gain, ffn_pre_gain, w_up, conv_w, conv_b, w_down,
              ffn_post_gain)
    tabs = _rope_tables(s)
    mlp = {"w_up": w_up.astype(BF16), "w_down": w_down.astype(BF16),
           "conv_w": conv_w.astype(F32), "conv_b": conv_b.astype(F32)[:, None, :]}
    for l in range(w_in.shape[0]):
        lw = _layer_weights(l, *params)
        (a_q, a_k, a_v, b_qT, b_k, b_vT, c_qT, c_k, c_vT, d_q, d_k, d_v) = _projection(
            x, lw, tabs, TOKEN_TILE)
        o_a = _neighbourhood(a_q, a_k, a_v, lw["na_bias"])
        o_bT = _flash(b_qT, b_k, b_vT, n_heads=GROUP_HEADS, n_kv=GROUP_HEADS, dk=LANES,
                      k_head_major=False, tq=FLASH_TQ * FLASH_STREAMS * FLASH_UNITS,
                      n_streams=FLASH_STREAMS)
        o_cT = _flash(c_qT, c_k, c_vT, n_heads=GROUP_HEADS, n_kv=GROUP_KV_HEADS, dk=HEAD_DIM,
                      k_head_major=True, tq=FLASH_TQ * FLASH_STREAMS * FLASH_UNITS,
                      n_streams=FLASH_STREAMS)
        o_d = _sliding_window(d_q, d_k, d_v, lw["sink"], SW_TQ)
        x = _out_projection(x, o_a, o_bT, o_cT, o_d, lw["w_out"], lw["post_gain"], TOKEN_TILE)
        x = _ffn(x, lw, mlp, l, FFN_TILE)
    return x
```
